```python
import jax, jax.numpy as jnp
from jax import lax
import numpy as np

D_MODEL = 4096
BATCH = 16
SEQ = 256
DEPTH = 1
DEC_BATCH = 2
DEC_SEQ = 2048
PAST_LEN = 256

GRID_W = 64
HEAD_DIM = 128
N_Q_HEADS = 16
N_KV_HEADS = 4
GQA_GROUP = N_Q_HEADS // N_KV_HEADS
ATTN_WIDTH = N_Q_HEADS * HEAD_DIM
KV_WIDTH = N_KV_HEADS * HEAD_DIM
CONV_WIDTH = D_MODEL - ATTN_WIDTH
CONV_K = 3
IN_WIDTH = ATTN_WIDTH + 2 * KV_WIDTH + 3 * CONV_WIDTH
WINDOW = 128
BLOCK = 128
D_FF = 11008
ROPE_THETA = 10000.0
ROPE_AXIS_DIM = HEAD_DIM // 2
ROPE_PAIRS = ROPE_AXIS_DIM // 2
N_SUB = 3
EPS = 1e-6
NEG = -1e30

kernel_name = "hybrid_swa_shortconv_macaron_dit_step"


def _rmsnorm(x, g):
    xf = x.astype(jnp.float32)
    y = xf * lax.rsqrt(jnp.mean(xf * xf, axis=-1, keepdims=True) + EPS)
    return (y * g.astype(jnp.float32)).astype(x.dtype)


def _swiglu(u, wg, wu, wd):
    return (jax.nn.silu(u @ wg) * (u @ wu)) @ wd


def _modulation(cond, w_mod_l, b_mod_l):
    m = jax.nn.silu(cond) @ w_mod_l + b_mod_l
    return m.reshape(m.shape[0], N_SUB, 3, D_MODEL)[:, :, :, None, :]


def _pre(h, m, s, g_pre):
    return _rmsnorm(h, g_pre) * (1 + m[:, s, 1]) + m[:, s, 0]


def _post(h, o, m, s, g_post, res_w):
    return h + res_w * m[:, s, 2] * _rmsnorm(o, g_post)


def _rotate_axis(xa, pos):
    inv_freq = ROPE_THETA ** (-jnp.arange(ROPE_PAIRS, dtype=jnp.float32) / ROPE_PAIRS)
    ang = pos.astype(jnp.float32)[:, None] * inv_freq[None, :]
    extra = xa.ndim - 3
    ang = ang.reshape(ang.shape[0], *([1] * extra), ROPE_PAIRS)
    cos, sin = jnp.cos(ang), jnp.sin(ang)
    xf = xa.astype(jnp.float32)
    x1, x2 = xf[..., :ROPE_PAIRS], xf[..., ROPE_PAIRS:]
    out = jnp.concatenate([x1 * cos - x2 * sin, x2 * cos + x1 * sin], axis=-1)
    return out.astype(xa.dtype)


def _rope_2d(x):
    T = x.shape[1]
    rows = T // GRID_W
    row_ids = jnp.repeat(jnp.arange(rows), GRID_W)
    col_ids = jnp.tile(jnp.arange(GRID_W), rows)
    return jnp.concatenate([_rotate_axis(x[..., :ROPE_AXIS_DIM], row_ids),
                            _rotate_axis(x[..., ROPE_AXIS_DIM:], col_ids)], axis=-1)


def _attend(q, k, v, mask, sink):
    s = jnp.einsum('bqhgd,bkhd->bhgqk', q, k, preferred_element_type=jnp.float32) * (HEAD_DIM ** -0.5)
    if mask is not None:
        s = jnp.where(mask, s, NEG)
    sink_col = jnp.broadcast_to(sink.astype(jnp.float32).reshape(1, N_KV_HEADS, GQA_GROUP, 1, 1),
                                s.shape[:-1] + (1,))
    p = jax.nn.softmax(jnp.concatenate([s, sink_col], axis=-1), axis=-1)[..., :-1]
    return jnp.einsum('bhgqk,bkhd->bqhgd', p.astype(v.dtype), v)


def _to_blocks(q):
    B, T = q.shape[:2]
    return q.reshape(B, T // BLOCK, BLOCK, N_KV_HEADS, GQA_GROUP, HEAD_DIM).transpose(1, 0, 2, 3, 4, 5)


def _from_blocks(ob):
    nb, B = ob.shape[:2]
    return ob.transpose(1, 0, 2, 3, 4, 5).reshape(B, nb * BLOCK, ATTN_WIDTH)


def _context_attention(q, k, v, sink):
    ob = lax.map(lambda qi: _attend(qi, k, v, None, sink), _to_blocks(q))
    return _from_blocks(ob)


def _latent_attention(q, k, v, k_ctx, v_ctx, sink):
    T = q.shape[1]
    nb = T // BLOCK
    P = k_ctx.shape[1]
    pad = ((0, 0), (BLOCK, BLOCK), (0, 0), (0, 0))
    k_pad, v_pad = jnp.pad(k, pad), jnp.pad(v, pad)
    q_off = jnp.arange(BLOCK)
    k_off = jnp.arange(3 * BLOCK) - BLOCK
    in_window = jnp.abs(k_off[None, :] - q_off[:, None]) <= WINDOW
    ctx_mask = jnp.ones((BLOCK, P), dtype=bool)

    def step(args):
        blk, qi = args
        kw = lax.dynamic_slice_in_dim(k_pad, blk * BLOCK, 3 * BLOCK, axis=1)
        vw = lax.dynamic_slice_in_dim(v_pad, blk * BLOCK, 3 * BLOCK, axis=1)
        key_pos = blk * BLOCK + k_off
        valid = in_window & ((key_pos >= 0) & (key_pos < T))[None, :]
        keys = jnp.concatenate([kw, k_ctx], axis=1)
        vals = jnp.concatenate([vw, v_ctx], axis=1)
        mask = jnp.concatenate([valid, ctx_mask], axis=1)
        return _attend(qi, keys, vals, mask, sink)

    ob = lax.map(step, (jnp.arange(nb), _to_blocks(q)))
    return _from_blocks(ob)


def _project(u, w_in_l):
    B, T = u.shape[:2]
    z = u @ w_in_l
    o1 = ATTN_WIDTH
    o2 = o1 + KV_WIDTH
    o3 = o2 + KV_WIDTH
    o4 = o3 + CONV_WIDTH
    o5 = o4 + CONV_WIDTH
    q = z[..., :o1].reshape(B, T, N_KV_HEADS, GQA_GROUP, HEAD_DIM)
    k = z[..., o1:o2].reshape(B, T, N_KV_HEADS, HEAD_DIM)
    v = z[..., o2:o3].reshape(B, T, N_KV_HEADS, HEAD_DIM)
    return q, k, v, z[..., o3:o4], z[..., o4:o5], z[..., o5:]


def _short_conv(gb, gc, hc, w_conv_l):
    u = gc * hc
    up = jnp.pad(u, ((0, 0), (1, 1), (0, 0)))
    y = up[:, :-2] * w_conv_l[0] + up[:, 1:-1] * w_conv_l[1] + up[:, 2:] * w_conv_l[2]
    return gb * y


def _merge(attn_o, conv_o, g_a, g_c, w_o_l):
    return jnp.concatenate([_rmsnorm(attn_o, g_a), _rmsnorm(conv_o, g_c)], axis=-1) @ w_o_l


def setup_inputs(seed: int = 0) -> dict:
    key = jax.random.key(seed)
    ks = jax.random.split(key, 24)
    f32 = jnp.float32
    nrm = lambda k, shape, s: jax.random.normal(k, shape, f32) * s
    return {
        "x_prompt": nrm(ks[0], (BATCH, SEQ, D_MODEL), 1.0),
        "x_sample": nrm(ks[1], (DEC_BATCH, DEC_SEQ, D_MODEL), 1.0),
        "c": nrm(ks[2], (DEC_BATCH, D_MODEL), 1.0),
        "cache_k": nrm(ks[3], (DEC_BATCH, DEPTH, PAST_LEN, N_KV_HEADS, HEAD_DIM), 1.0),
        "cache_v": nrm(ks[4], (DEC_BATCH, DEPTH, PAST_LEN, N_KV_HEADS, HEAD_DIM), 1.0),
        "c_ctx": nrm(ks[5], (D_MODEL,), 1.0),
        "w_mod": nrm(ks[6], (DEPTH, D_MODEL, N_SUB * 3 * D_MODEL), 0.5 * D_MODEL ** -0.5),
        "b_mod": nrm(ks[7], (DEPTH, N_SUB * 3 * D_MODEL), 0.02),
        "g_pre": 1.0 + nrm(ks[8], (DEPTH, N_SUB, D_MODEL), 0.05),
        "g_post": 1.0 + nrm(ks[9], (DEPTH, N_SUB, D_MODEL), 0.05),
        "w_in": nrm(ks[10], (DEPTH, D_MODEL, IN_WIDTH), D_MODEL ** -0.5),
        "w_conv": nrm(ks[11], (DEPTH, CONV_K, CONV_WIDTH), CONV_K ** -0.5),
        "sink": nrm(ks[12], (DEPTH, N_Q_HEADS), 0.5),
        "g_attn_out": 1.0 + nrm(ks[13], (DEPTH, ATTN_WIDTH), 0.05),
        "g_conv_out": 1.0 + nrm(ks[14], (DEPTH, CONV_WIDTH), 0.05),
        "w_o": nrm(ks[15], (DEPTH, D_MODEL, D_MODEL), D_MODEL ** -0.5),
        "w_ffn1_gate": nrm(ks[16], (DEPTH, D_MODEL, D_FF), D_MODEL ** -0.5),
        "w_ffn1_up": nrm(ks[17], (DEPTH, D_MODEL, D_FF), D_MODEL ** -0.5),
        "w_ffn1_down": nrm(ks[18], (DEPTH, D_FF, D_MODEL), D_FF ** -0.5),
        "w_ffn2_gate": nrm(ks[19], (DEPTH, D_MODEL, D_FF), D_MODEL ** -0.5),
        "w_ffn2_up": nrm(ks[20], (DEPTH, D_MODEL, D_FF), D_MODEL ** -0.5),
        "w_ffn2_down": nrm(ks[21], (DEPTH, D_FF, D_MODEL), D_FF ** -0.5),
    }


def reference(x_prompt, x_sample, c, cache_k, cache_v, c_ctx, w_mod, b_mod, g_pre, g_post,
              w_in, w_conv, sink, g_attn_out, g_conv_out, w_o,
              w_ffn1_gate, w_ffn1_up, w_ffn1_down, w_ffn2_gate, w_ffn2_up, w_ffn2_down):
    h = x_prompt
    ks_new, vs_new = [], []
    for l in range(DEPTH):
        m = _modulation(c_ctx[None, :], w_mod[l], b_mod[l])
        o = _swiglu(_pre(h, m, 0, g_pre[l, 0]), w_ffn1_gate[l], w_ffn1_up[l], w_ffn1_down[l])
        h = _post(h, o, m, 0, g_post[l, 0], 0.5)
        q, k, v, gb, gc, hc = _project(_pre(h, m, 1, g_pre[l, 1]), w_in[l])
        attn_o = _context_attention(q, k, v, sink[l])
        conv_o = _short_conv(gb, gc, hc, w_conv[l])
        o = _merge(attn_o, conv_o, g_attn_out[l], g_conv_out[l], w_o[l])
        h = _post(h, o, m, 1, g_post[l, 1], 1.0)
        o = _swiglu(_pre(h, m, 2, g_pre[l, 2]), w_ffn2_gate[l], w_ffn2_up[l], w_ffn2_down[l])
        h = _post(h, o, m, 2, g_post[l, 2], 0.5)
        ks_new.append(k)
        vs_new.append(v)
    y_prompt = h
    state_k = jnp.stack(ks_new, axis=1)
    state_v = jnp.stack(vs_new, axis=1)

    h = x_sample
    for l in range(DEPTH):
        m = _modulation(c, w_mod[l], b_mod[l])
        o = _swiglu(_pre(h, m, 0, g_pre[l, 0]), w_ffn1_gate[l], w_ffn1_up[l], w_ffn1_down[l])
        h = _post(h, o, m, 0, g_post[l, 0], 0.5)
        q, k, v, gb, gc, hc = _project(_pre(h, m, 1, g_pre[l, 1]), w_in[l])
        q, k = _rope_2d(q), _rope_2d(k)
        attn_o = _latent_attention(q, k, v, cache_k[:, l], cache_v[:, l], sink[l])
        conv_o = _short_conv(gb, gc, hc, w_conv[l])
        o = _merge(attn_o, conv_o, g_attn_out[l], g_conv_out[l], w_o[l])
        h = _post(h, o, m, 1, g_post[l, 1], 1.0)
        o = _swiglu(_pre(h, m, 2, g_pre[l, 2]), w_ffn2_gate[l], w_ffn2_up[l], w_ffn2_down[l])
        h = _post(h, o, m, 2, g_post[l, 2], 0.5)
    y_sample = h
    return (y_prompt, y_sample, state_k, state_v)
```

```python
import functools

import jax
import jax.numpy as jnp
from jax import lax
from jax.experimental import pallas as pl
from jax.experimental.pallas import tpu as pltpu

F32 = jnp.float32
BF16 = jnp.bfloat16

N_Q_HEADS = 16
N_KV_HEADS = 4
GQA_GROUP = N_Q_HEADS // N_KV_HEADS
HEAD_DIM = 128
GRID_W = 64
WINDOW = 128
ROPE_THETA = 10000.0
ROPE_AXIS_DIM = HEAD_DIM // 2
ROPE_PAIRS = ROPE_AXIS_DIM // 2
N_SUB = 3
EPS = 1e-6
NEG = -1e30
ATTN_SCALE = HEAD_DIM ** -0.5

VMEM_LIMIT_BYTES = 56 * 1024 * 1024
MXU_COLS = 256
ROW_TILE = 256
MM_ROWS = 1024
DOWN_N_CHUNK = 512


def _params(*sem):
    return pltpu.CompilerParams(dimension_semantics=sem, vmem_limit_bytes=VMEM_LIMIT_BYTES)


def _rms(x, g):
    return x * lax.rsqrt(jnp.mean(x * x, axis=-1, keepdims=True) + EPS) * g


def _mod_kernel(c_ref, w_ref, b_ref, o_ref):
    x = c_ref[...]
    x = (x * jax.nn.sigmoid(x)).astype(BF16)
    o_ref[...] = jnp.dot(x, w_ref[...].astype(BF16), preferred_element_type=F32) + b_ref[...]


def _modulation(cond, w_mod, b_mod):
    rows, d = cond.shape
    n = w_mod.shape[1]
    tn = 512
    return pl.pallas_call(
        _mod_kernel,
        grid=(n // tn,),
        in_specs=[pl.BlockSpec((rows, d), lambda j: (0, 0)),
                  pl.BlockSpec((d, tn), lambda j: (0, j)),
                  pl.BlockSpec((1, tn), lambda j: (0, j))],
        out_specs=pl.BlockSpec((rows, tn), lambda j: (0, j)),
        out_shape=jax.ShapeDtypeStruct((rows, n), F32),
        compiler_params=_params("arbitrary"),
        name="modulation",
    )(cond, w_mod, b_mod.reshape(1, n))


def _pre_kernel(s, h_ref, m_ref, gpre_ref, u_ref):
    shift = m_ref[pl.ds(3 * s, 1), :]
    scale = m_ref[pl.ds(3 * s + 1, 1), :]
    u = _rms(h_ref[...], gpre_ref[pl.ds(s, 1), :]) * (1.0 + scale) + shift
    u_ref[...] = u.astype(u_ref.dtype)


def _post_kernel(s, res_w, with_pre, h_ref, o_ref, m_ref, gpost_ref, gpre_ref, hn_ref, *u_ref):
    gate = m_ref[pl.ds(3 * s + 2, 1), :]
    hn = h_ref[...] + (res_w * gate) * _rms(o_ref[...], gpost_ref[pl.ds(s, 1), :])
    hn_ref[...] = hn
    if with_pre:
        shift = m_ref[pl.ds(3 * (s + 1), 1), :]
        scale = m_ref[pl.ds(3 * (s + 1) + 1, 1), :]
        u = _rms(hn, gpre_ref[pl.ds(s + 1, 1), :]) * (1.0 + scale) + shift
        u_ref[0][...] = u.astype(BF16)


def _group_of_tile(i, tile_rows, n_ctx, lat_seq):
    r0 = i * tile_rows
    return jnp.where(r0 < n_ctx, 0, 1 + (r0 - n_ctx) // lat_seq)


def _pre(s, h, mod3, g_pre, n_ctx, lat_seq):
    n, d = h.shape
    grp = functools.partial(_group_of_tile, tile_rows=ROW_TILE, n_ctx=n_ctx, lat_seq=lat_seq)
    return pl.pallas_call(
        functools.partial(_pre_kernel, s),
        grid=(n // ROW_TILE,),
        in_specs=[pl.BlockSpec((ROW_TILE, d), lambda i: (i, 0)),
                  pl.BlockSpec((None, 3 * N_SUB, d), lambda i: (grp(i), 0, 0)),
                  pl.BlockSpec((N_SUB, d), lambda i: (0, 0))],
        out_specs=pl.BlockSpec((ROW_TILE, d), lambda i: (i, 0)),
        out_shape=jax.ShapeDtypeStruct((n, d), BF16),
        compiler_params=_params("parallel"),
        name=f"pre{s}",
    )(h, mod3, g_pre)


def _post(s, res_w, h, o, mod3, g_post, g_pre, n_ctx, lat_seq, with_pre, row0=0, rows=None):
    n, d = h.shape
    rows = n if rows is None else rows
    t0 = row0 // ROW_TILE
    grp = lambda i: _group_of_tile(i + t0, ROW_TILE, n_ctx, lat_seq)
    out_shape = [jax.ShapeDtypeStruct((rows, d), F32)]
    out_specs = [pl.BlockSpec((ROW_TILE, d), lambda i: (i, 0))]
    if with_pre:
        out_shape.append(jax.ShapeDtypeStruct((rows, d), BF16))
        out_specs.append(pl.BlockSpec((ROW_TILE, d), lambda i: (i, 0)))
    return pl.pallas_call(
        functools.partial(_post_kernel, s, res_w, with_pre),
        grid=(rows // ROW_TILE,),
        in_specs=[pl.BlockSpec((ROW_TILE, d), lambda i: (i + t0, 0)),
                  pl.BlockSpec((ROW_TILE, d), lambda i: (i + t0, 0)),
                  pl.BlockSpec((None, 3 * N_SUB, d), lambda i: (grp(i), 0, 0)),
                  pl.BlockSpec((N_SUB, d), lambda i: (0, 0)),
                  pl.BlockSpec((N_SUB, d), lambda i: (0, 0))],
        out_specs=out_specs,
        out_shape=out_shape,
        compiler_params=_params("parallel"),
        name=f"post{s}",
    )(h, o, mod3, g_post, g_pre)


def _mm_kernel(n_w, combine, x_ref, *refs):
    w_refs, o_ref, wb_refs = refs[:n_w], refs[n_w], refs[n_w + 1:]

    @pl.when(pl.program_id(1) == 0)
    def _():
        for w_ref, wb_ref in zip(w_refs, wb_refs):
            wb_ref[...] = w_ref[...].astype(BF16)

    x = x_ref[...]
    ys = [jnp.dot(x, wb_ref[...], preferred_element_type=F32) for wb_ref in wb_refs]
    o_ref[...] = combine(*ys).astype(o_ref.dtype)


def _matmul(x, ws, combine, out_dtype, n_out, col0=0, name="mm"):
    m, k = x.shape
    tn = MXU_COLS
    c0 = col0 // tn
    n_w = len(ws)
    return pl.pallas_call(
        functools.partial(_mm_kernel, n_w, combine),
        grid=(n_out // tn, m // MM_ROWS),
        in_specs=[pl.BlockSpec((MM_ROWS, k), lambda j, i: (i, 0))]
                 + [pl.BlockSpec((k, tn), lambda j, i: (0, j + c0))] * n_w,
        out_specs=pl.BlockSpec((MM_ROWS, tn), lambda j, i: (i, j)),
        out_shape=jax.ShapeDtypeStruct((m, n_out), out_dtype),
        scratch_shapes=[pltpu.VMEM((k, tn), BF16)] * n_w,
        compiler_params=_params("arbitrary", "arbitrary"),
        name=name,
    )(x, *ws)


def _swiglu_combine(g, u):
    return (g * jax.nn.sigmoid(g)) * u


def _down_kernel(a_ref, w_ref, o_ref):
    k = pl.program_id(1)
    a = a_ref[...]
    n = o_ref.shape[1]
    for c in range(0, n, DOWN_N_CHUNK):
        y = jnp.dot(a, w_ref[:, c:c + DOWN_N_CHUNK].astype(BF16), preferred_element_type=F32)

        @pl.when(k == 0)
        def _():
            o_ref[:, c:c + DOWN_N_CHUNK] = y

        @pl.when(k > 0)
        def _():
            o_ref[:, c:c + DOWN_N_CHUNK] += y


def _down(a, w):
    m, f = a.shape
    d = w.shape[1]
    tk = MXU_COLS
    return pl.pallas_call(
        _down_kernel,
        grid=(m // MM_ROWS, f // tk),
        in_specs=[pl.BlockSpec((MM_ROWS, tk), lambda i, k: (i, k)),
                  pl.BlockSpec((tk, d), lambda i, k: (k, 0))],
        out_specs=pl.BlockSpec((MM_ROWS, d), lambda i, k: (i, 0)),
        out_shape=jax.ShapeDtypeStruct((m, d), F32),
        compiler_params=_params("parallel", "arbitrary"),
        name="ffn_down",
    )(a, w)


def _softmax_with_sink(s, sink):
    m = jnp.maximum(jnp.max(s, axis=-1, keepdims=True), sink)
    p = jnp.exp(s - m)
    denom = jnp.sum(p, axis=-1, keepdims=True) + jnp.exp(sink - m)
    return p / denom


def _qk(q, k):
    return lax.dot_general(q, k, (((1,), (1,)), ((), ())), preferred_element_type=F32) * ATTN_SCALE


def _ctx_attn_kernel(sink_ref, q_ref, k_ref, v_ref, o_ref):
    h = pl.program_id(1)
    k = k_ref[...].astype(BF16)
    v = v_ref[...].astype(BF16)
    for g in range(GQA_GROUP):
        cols = slice(g * HEAD_DIM, (g + 1) * HEAD_DIM)
        p = _softmax_with_sink(_qk(q_ref[:, cols].astype(BF16), k), sink_ref[h * GQA_GROUP + g])
        o_ref[:, cols] = jnp.dot(p.astype(BF16), v, preferred_element_type=F32)


def _ctx_attention(z, sink, n_batch, seq):
    qw = GQA_GROUP * HEAD_DIM
    k_col = N_Q_HEADS * HEAD_DIM // HEAD_DIM
    v_col = k_col + N_KV_HEADS
    return pl.pallas_call(
        _ctx_attn_kernel,
        grid=(n_batch, N_KV_HEADS),
        in_specs=[pl.BlockSpec(memory_space=pltpu.SMEM),
                  pl.BlockSpec((seq, qw), lambda b, h: (b, h)),
                  pl.BlockSpec((seq, HEAD_DIM), lambda b, h: (b, k_col + h)),
                  pl.BlockSpec((seq, HEAD_DIM), lambda b, h: (b, v_col + h))],
        out_specs=pl.BlockSpec((seq, qw), lambda b, h: (b, h)),
        out_shape=jax.ShapeDtypeStruct((n_batch * seq, N_Q_HEADS * HEAD_DIM), F32),
        compiler_params=_params("parallel", "parallel"),
        name="ctx_attention",
    )(sink, z, z, z)


def _rope(x, cos, sin_signed):
    lane = lax.broadcasted_iota(jnp.int32, x.shape, 1)
    first = (lane % ROPE_AXIS_DIM) < ROPE_PAIRS
    partner = jnp.where(first, pltpu.roll(x, HEAD_DIM - ROPE_PAIRS, axis=1),
                        pltpu.roll(x, ROPE_PAIRS, axis=1))
    return x * cos + partner * sin_signed


def _lat_attn_kernel(sink_ref, q_ref, k_ref, v_ref, kc_ref, vc_ref, cos_ref, sin_ref, o_ref,
                     kr_ref, vb_ref):
    h = pl.program_id(1)
    i = pl.program_id(2)
    n_blk = pl.num_programs(2)
    blk = q_ref.shape[0]

    @pl.when(i == 0)
    def _():
        kr_ref[...] = _rope(k_ref[...], cos_ref[...], sin_ref[...]).astype(BF16)
        vb_ref[...] = v_ref[...].astype(BF16)

    q0 = pl.multiple_of(i * blk, blk)
    w0 = pl.multiple_of(jnp.clip(i - 1, 0, n_blk - 3) * blk, blk)
    kw = kr_ref[pl.ds(w0, 3 * blk), :]
    vw = vb_ref[pl.ds(w0, 3 * blk), :]
    kc = kc_ref[...].astype(BF16)
    vc = vc_ref[...].astype(BF16)
    cos = cos_ref[pl.ds(q0, blk), :]
    sin = sin_ref[pl.ds(q0, blk), :]
    q_pos = q0 + lax.broadcasted_iota(jnp.int32, (blk, 3 * blk), 0)
    k_pos = w0 + lax.broadcasted_iota(jnp.int32, (blk, 3 * blk), 1)
    in_window = jnp.abs(k_pos - q_pos) <= WINDOW
    for g in range(GQA_GROUP):
        cols = slice(g * HEAD_DIM, (g + 1) * HEAD_DIM)
        q = _rope(q_ref[:, cols], cos, sin).astype(BF16)
        s = jnp.concatenate([jnp.where(in_window, _qk(q, kw), NEG), _qk(q, kc)], axis=-1)
        p = _softmax_with_sink(s, sink_ref[h * GQA_GROUP + g]).astype(BF16)
        o_ref[:, cols] = (jnp.dot(p[:, :3 * blk], vw, preferred_element_type=F32)
                          + jnp.dot(p[:, 3 * blk:], vc, preferred_element_type=F32))


def _lat_attention(z, cache_k, cache_v, sink, cos, sin_signed, row0, n_batch, seq):
    blk = WINDOW
    n_blk = seq // blk
    past = cache_k.shape[1]
    qw = GQA_GROUP * HEAD_DIM
    k_col = N_Q_HEADS
    v_col = k_col + N_KV_HEADS
    return pl.pallas_call(
        _lat_attn_kernel,
        grid=(n_batch, N_KV_HEADS, n_blk),
        in_specs=[pl.BlockSpec(memory_space=pltpu.SMEM),
                  pl.BlockSpec((blk, qw), lambda b, h, i: (row0 // blk + b * n_blk + i, h)),
                  pl.BlockSpec((seq, HEAD_DIM), lambda b, h, i: (row0 // seq + b, k_col + h)),
                  pl.BlockSpec((seq, HEAD_DIM), lambda b, h, i: (row0 // seq + b, v_col + h)),
                  pl.BlockSpec((None, past, HEAD_DIM), lambda b, h, i: (b, 0, h)),
                  pl.BlockSpec((None, past, HEAD_DIM), lambda b, h, i: (b, 0, h)),
                  pl.BlockSpec((seq, HEAD_DIM), lambda b, h, i: (0, 0)),
                  pl.BlockSpec((seq, HEAD_DIM), lambda b, h, i: (0, 0))],
        out_specs=pl.BlockSpec((blk, qw), lambda b, h, i: (b * n_blk + i, h)),
        out_shape=jax.ShapeDtypeStruct((n_batch * seq, N_Q_HEADS * HEAD_DIM), F32),
        scratch_shapes=[pltpu.VMEM((seq, HEAD_DIM), BF16), pltpu.VMEM((seq, HEAD_DIM), BF16)],
        compiler_params=_params("parallel", "parallel", "arbitrary"),
        name="latent_attention",
    )(sink, z, z, z, cache_k, cache_v, cos, sin_signed)


def _rope_tables(seq):
    t = jnp.arange(seq)
    pos = jnp.stack([t // GRID_W, t % GRID_W], axis=1).astype(F32)
    inv_freq = ROPE_THETA ** (-jnp.arange(ROPE_PAIRS, dtype=F32) / ROPE_PAIRS)
    ang = pos[:, :, None] * inv_freq[None, None, :]
    cos, sin = jnp.cos(ang), jnp.sin(ang)
    cos = jnp.concatenate([cos, cos], axis=-1).reshape(seq, HEAD_DIM)
    sin = jnp.concatenate([-sin, sin], axis=-1).reshape(seq, HEAD_DIM)
    return cos, sin


def _conv_kernel(n_ctx, lat_seq, gb_ref, gc_ref, hc_ref, gcp_ref, hcp_ref, gcn_ref, hcn_ref,
                 w_ref, o_ref):
    t = pl.program_id(0)
    rows = gb_ref.shape[0]
    r0 = t * rows
    is_first = jnp.logical_or(r0 < n_ctx, (r0 - n_ctx) % lat_seq == 0)
    is_last = jnp.logical_or(r0 < n_ctx, (r0 + rows - n_ctx) % lat_seq == 0)
    u = gc_ref[...] * hc_ref[...]
    halo_rows = gcp_ref.shape[0]
    u_before = jnp.where(is_first, 0.0, gcp_ref[pl.ds(halo_rows - 1, 1), :] * hcp_ref[pl.ds(halo_rows - 1, 1), :])
    u_after = jnp.where(is_last, 0.0, gcn_ref[pl.ds(0, 1), :] * hcn_ref[pl.ds(0, 1), :])
    row = lax.broadcasted_iota(jnp.int32, u.shape, 0)
    u_prev = jnp.where(row == 0, u_before, pltpu.roll(u, 1, axis=0))
    u_next = jnp.where(row == rows - 1, u_after, pltpu.roll(u, rows - 1, axis=0))
    y = u_prev * w_ref[pl.ds(0, 1), :] + u * w_ref[pl.ds(1, 1), :] + u_next * w_ref[pl.ds(2, 1), :]
    o_ref[...] = gb_ref[...] * y


def _short_conv(z, w_conv, col0, width, n_ctx, ctx_seq, lat_seq):
    assert ctx_seq == ROW_TILE
    n = z.shape[0]
    tc = 1024
    halo = 8
    nc = width // tc
    b0, c0, h0 = col0 // tc, (col0 + width) // tc, (col0 + 2 * width) // tc
    per = ROW_TILE // halo
    last_halo = n // halo - 1
    prev_map = lambda off: (lambda t, c: (jnp.maximum(t * per - 1, 0), off + c))
    next_map = lambda off: (lambda t, c: (jnp.minimum((t + 1) * per, last_halo), off + c))
    return pl.pallas_call(
        functools.partial(_conv_kernel, n_ctx, lat_seq),
        grid=(n // ROW_TILE, nc),
        in_specs=[pl.BlockSpec((ROW_TILE, tc), lambda t, c: (t, b0 + c)),
                  pl.BlockSpec((ROW_TILE, tc), lambda t, c: (t, c0 + c)),
                  pl.BlockSpec((ROW_TILE, tc), lambda t, c: (t, h0 + c)),
                  pl.BlockSpec((halo, tc), prev_map(c0)),
                  pl.BlockSpec((halo, tc), prev_map(h0)),
                  pl.BlockSpec((halo, tc), next_map(c0)),
                  pl.BlockSpec((halo, tc), next_map(h0)),
                  pl.BlockSpec((w_conv.shape[0], tc), lambda t, c: (0, c))],
        out_specs=pl.BlockSpec((ROW_TILE, tc), lambda t, c: (t, c)),
        out_shape=jax.ShapeDtypeStruct((n, width), F32),
        compiler_params=_params("parallel", "parallel"),
        name="short_conv",
    )(z, z, z, z, z, z, z, w_conv)


def _merge_kernel(a_ref, c_ref, ga_ref, gc_ref, o_ref):
    wa = a_ref.shape[1]
    o_ref[:, :wa] = _rms(a_ref[...], ga_ref[...]).astype(BF16)
    o_ref[:, wa:] = _rms(c_ref[...], gc_ref[...]).astype(BF16)


def _merge(attn_ctx, attn_lat, conv_o, g_a, g_c):
    n, wc = conv_o.shape
    wa = attn_ctx.shape[1]
    t_ctx = attn_ctx.shape[0] // ROW_TILE
    t_lat = attn_lat.shape[0] // ROW_TILE

    def kern(ac_ref, al_ref, c_ref, ga_ref, gc_ref, o_ref):
        is_ctx = pl.program_id(0) < t_ctx

        @pl.when(is_ctx)
        def _():
            _merge_kernel(ac_ref, c_ref, ga_ref, gc_ref, o_ref)

        @pl.when(jnp.logical_not(is_ctx))
        def _():
            _merge_kernel(al_ref, c_ref, ga_ref, gc_ref, o_ref)

    return pl.pallas_call(
        kern,
        grid=(n // ROW_TILE,),
        in_specs=[pl.BlockSpec((ROW_TILE, wa), lambda i: (jnp.minimum(i, t_ctx - 1), 0)),
                  pl.BlockSpec((ROW_TILE, wa), lambda i: (jnp.clip(i - t_ctx, 0, t_lat - 1), 0)),
                  pl.BlockSpec((ROW_TILE, wc), lambda i: (i, 0)),
                  pl.BlockSpec((1, wa), lambda i: (0, 0)),
                  pl.BlockSpec((1, wc), lambda i: (0, 0))],
        out_specs=pl.BlockSpec((ROW_TILE, wa + wc), lambda i: (i, 0)),
        out_shape=jax.ShapeDtypeStruct((n, wa + wc), BF16),
        compiler_params=_params("parallel"),
        name="merge_norm",
    )(attn_ctx, attn_lat, conv_o, g_a.reshape(1, wa), g_c.reshape(1, wc))


def kernel(x_prompt, x_sample, c, cache_k, cache_v, c_ctx, w_mod, b_mod, g_pre, g_post, w_in, w_conv,
           sink, g_attn_out, g_conv_out, w_o, w_ffn1_gate, w_ffn1_up, w_ffn1_down,
           w_ffn2_gate, w_ffn2_up, w_ffn2_down):
    batch, seq, d = x_prompt.shape
    dec_batch, dec_seq, _ = x_sample.shape
    depth = w_mod.shape[0]
    past = cache_k.shape[2]
    n_ctx = batch * seq
    n_lat = dec_batch * dec_seq
    attn_w = N_Q_HEADS * HEAD_DIM
    kv_w = N_KV_HEADS * HEAD_DIM
    conv_w = d - attn_w
    d_ff = w_ffn1_gate.shape[2]

    h = jnp.concatenate([x_prompt.reshape(n_ctx, d), x_sample.reshape(n_lat, d)], axis=0)
    cond = jnp.concatenate([c_ctx[None, :], c, jnp.zeros((8 - 1 - dec_batch, d), F32)], axis=0)
    cos, sin_signed = _rope_tables(dec_seq)
    post = functools.partial(_post, n_ctx=n_ctx, lat_seq=dec_seq)

    ks_new, vs_new = [], []
    for l in range(depth):
        mod3 = _modulation(cond, w_mod[l], b_mod[l]).reshape(8, 3 * N_SUB, d)
        u = _pre(0, h, mod3, g_pre[l], n_ctx, dec_seq)

        a = _matmul(u, [w_ffn1_gate[l], w_ffn1_up[l]], _swiglu_combine, BF16, d_ff, name="ffn_up")
        o = _down(a, w_ffn1_down[l])
        h, u = post(0, 0.5, h, o, mod3, g_post[l], g_pre[l], with_pre=True)

        z = _matmul(u, [w_in[l]], lambda y: y, F32, w_in.shape[2], name="in_proj")
        attn_ctx = _ctx_attention(z, sink[l], batch, seq)
        attn_lat = _lat_attention(z, cache_k[:, l].reshape(dec_batch, past, kv_w),
                                  cache_v[:, l].reshape(dec_batch, past, kv_w),
                                  sink[l], cos, sin_signed, n_ctx, dec_batch, dec_seq)
        conv_o = _short_conv(z, w_conv[l], attn_w + 2 * kv_w, conv_w, n_ctx, seq, dec_seq)
        mrg = _merge(attn_ctx, attn_lat, conv_o, g_attn_out[l], g_conv_out[l])
        o = _matmul(mrg, [w_o[l]], lambda y: y, F32, d, name="out_proj")
        h, u = post(1, 1.0, h, o, mod3, g_post[l], g_pre[l], with_pre=True)

        a = _matmul(u, [w_ffn2_gate[l], w_ffn2_up[l]], _swiglu_combine, BF16, d_ff, name="ffn_up")
        o = _down(a, w_ffn2_down[l])
        if l + 1 < depth:
            (h,) = post(2, 0.5, h, o, mod3, g_post[l], g_pre[l], with_pre=False)
        ks_new.append(z[:n_ctx, attn_w:attn_w + kv_w].reshape(batch, seq, N_KV_HEADS, HEAD_DIM))
        vs_new.append(z[:n_ctx, attn_w + kv_w:attn_w + 2 * kv_w].reshape(batch, seq, N_KV_HEADS, HEAD_DIM))

    (y_prompt,) = post(2, 0.5, h, o, mod3, g_post[depth - 1], g_pre[depth - 1], with_pre=False,
                       row0=0, rows=n_ctx)
    (y_sample,) = post(2, 0.5, h, o, mod3, g_post[depth - 1], g_pre[depth - 1], with_pre=False,
                       row0=n_ctx, rows=n_lat)
    return (y_prompt.reshape(batch, seq, d), y_sample.reshape(dec_batch, dec_seq, d),
            jnp.stack(ks_new, axis=1), jnp.stack(vs_new, axis=1))
```

```python
import functools

import jax
import jax.numpy as jnp
from jax import lax
from jax.experimental import pallas as pl
from jax.experimental.pallas import tpu as pltpu

F32 = jnp.float32
BF16 = jnp.bfloat16

N_Q_HEADS = 16
N_KV_HEADS = 4
GQA_GROUP = N_Q_HEADS // N_KV_HEADS
HEAD_DIM = 128
GRID_W = 64
WINDOW = 128
ROPE_THETA = 10000.0
ROPE_AXIS_DIM = HEAD_DIM // 2
ROPE_PAIRS = ROPE_AXIS_DIM // 2
N_SUB = 3
EPS = 1e-6
NEG = -1e30
ATTN_SCALE = HEAD_DIM ** -0.5

VMEM_LIMIT_BYTES = 56 * 1024 * 1024
MXU_COLS = 256
ROW_TILE = 256
MM_ROWS = 1024


def _params(*sem):
    return pltpu.CompilerParams(dimension_semantics=sem, vmem_limit_bytes=VMEM_LIMIT_BYTES)


def _rms(x, g):
    return x * lax.rsqrt(jnp.mean(x * x, axis=-1, keepdims=True) + EPS) * g


def _mod_kernel(c_ref, w_ref, b_ref, o_ref):
    x = c_ref[...]
    x = (x * jax.nn.sigmoid(x)).astype(BF16)
    o_ref[...] = jnp.dot(x, w_ref[...].astype(BF16), preferred_element_type=F32) + b_ref[...]


def _modulation(cond, w_mod, b_mod):
    rows, d = cond.shape
    n = w_mod.shape[1]
    tn = 512
    return pl.pallas_call(
        _mod_kernel,
        grid=(n // tn,),
        in_specs=[pl.BlockSpec((rows, d), lambda j: (0, 0)),
                  pl.BlockSpec((d, tn), lambda j: (0, j)),
                  pl.BlockSpec((1, tn), lambda j: (0, j))],
        out_specs=pl.BlockSpec((rows, tn), lambda j: (0, j)),
        out_shape=jax.ShapeDtypeStruct((rows, n), F32),
        compiler_params=_params("arbitrary"),
        name="modulation",
    )(cond, w_mod, b_mod.reshape(1, n))


def _load_rows(h_refs, n_ctx_tiles):
    if len(h_refs) == 1:
        return h_refs[0][...]
    return jnp.where(pl.program_id(0) < n_ctx_tiles, h_refs[0][...], h_refs[1][...])


def _pre_kernel(s, n_h, n_ctx_tiles, *refs):
    h_refs, (m_ref, gpre_ref, u_ref) = refs[:n_h], refs[n_h:]
    shift = m_ref[pl.ds(3 * s, 1), :]
    scale = m_ref[pl.ds(3 * s + 1, 1), :]
    u = _rms(_load_rows(h_refs, n_ctx_tiles), gpre_ref[pl.ds(s, 1), :]) * (1.0 + scale) + shift
    u_ref[...] = u.astype(u_ref.dtype)


def _post_kernel(s, res_w, with_pre, n_h, n_ctx_tiles, *refs):
    h_refs, (o_ref, m_ref, gpost_ref, gpre_ref, hn_ref, *u_ref) = refs[:n_h], refs[n_h:]
    gate = m_ref[pl.ds(3 * s + 2, 1), :]
    hn = _load_rows(h_refs, n_ctx_tiles) + (res_w * gate) * _rms(o_ref[...], gpost_ref[pl.ds(s, 1), :])
    hn_ref[...] = hn
    if with_pre:
        shift = m_ref[pl.ds(3 * (s + 1), 1), :]
        scale = m_ref[pl.ds(3 * (s + 1) + 1, 1), :]
        u = _rms(hn, gpre_ref[pl.ds(s + 1, 1), :]) * (1.0 + scale) + shift
        u_ref[0][...] = u.astype(BF16)


def _group_of_tile(i, tile_rows, n_ctx, lat_seq):
    r0 = i * tile_rows
    return jnp.where(r0 < n_ctx, 0, 1 + (r0 - n_ctx) // lat_seq)


def _row_specs(hs, d, t0=0):
    if len(hs) == 1:
        return [pl.BlockSpec((ROW_TILE, d), lambda i: (i + t0, 0))], 0
    t_ctx, t_lat = (x.shape[0] // ROW_TILE for x in hs)
    return [pl.BlockSpec((ROW_TILE, d), lambda i: (jnp.minimum(i, t_ctx - 1), 0)),
            pl.BlockSpec((ROW_TILE, d), lambda i: (jnp.clip(i - t_ctx, 0, t_lat - 1), 0))], t_ctx


def _pre(s, hs, mod3, g_pre, n_ctx, lat_seq):
    n = sum(x.shape[0] for x in hs)
    d = hs[0].shape[1]
    grp = functools.partial(_group_of_tile, tile_rows=ROW_TILE, n_ctx=n_ctx, lat_seq=lat_seq)
    h_specs, t_ctx = _row_specs(hs, d)
    return pl.pallas_call(
        functools.partial(_pre_kernel, s, len(hs), t_ctx),
        grid=(n // ROW_TILE,),
        in_specs=h_specs + [pl.BlockSpec((None, 3 * N_SUB, d), lambda i: (grp(i), 0, 0)),
                            pl.BlockSpec((N_SUB, d), lambda i: (0, 0))],
        out_specs=pl.BlockSpec((ROW_TILE, d), lambda i: (i, 0)),
        out_shape=jax.ShapeDtypeStruct((n, d), BF16),
        compiler_params=_params("parallel"),
        name=f"pre{s}",
    )(*hs, mod3, g_pre)


def _post(s, res_w, hs, o, mod3, g_post, g_pre, n_ctx, lat_seq, with_pre, row0=0, rows=None):
    d = o.shape[1]
    rows = o.shape[0] if rows is None else rows
    t0 = row0 // ROW_TILE
    grp = lambda i: _group_of_tile(i + t0, ROW_TILE, n_ctx, lat_seq)
    h_specs, t_ctx = _row_specs(hs, d, t0)
    out_shape = [jax.ShapeDtypeStruct((rows, d), F32)]
    out_specs = [pl.BlockSpec((ROW_TILE, d), lambda i: (i, 0))]
    if with_pre:
        out_shape.append(jax.ShapeDtypeStruct((rows, d), BF16))
        out_specs.append(pl.BlockSpec((ROW_TILE, d), lambda i: (i, 0)))
    return pl.pallas_call(
        functools.partial(_post_kernel, s, res_w, with_pre, len(hs), t_ctx),
        grid=(rows // ROW_TILE,),
        in_specs=h_specs + [pl.BlockSpec((ROW_TILE, d), lambda i: (i + t0, 0)),
                            pl.BlockSpec((None, 3 * N_SUB, d), lambda i: (grp(i), 0, 0)),
                            pl.BlockSpec((N_SUB, d), lambda i: (0, 0)),
                            pl.BlockSpec((N_SUB, d), lambda i: (0, 0))],
        out_specs=out_specs,
        out_shape=out_shape,
        compiler_params=_params("parallel"),
        name=f"post{s}",
    )(*hs, o, mod3, g_post, g_pre)


def _mm_kernel(n_w, combine, x_ref, *refs):
    w_refs, o_ref, wb_refs = refs[:n_w], refs[n_w], refs[n_w + 1:]

    @pl.when(pl.program_id(1) == 0)
    def _():
        for w_ref, wb_ref in zip(w_refs, wb_refs):
            wb_ref[...] = w_ref[...].astype(BF16)

    x = x_ref[...]
    ys = [jnp.dot(x, wb_ref[...], preferred_element_type=F32) for wb_ref in wb_refs]
    o_ref[...] = combine(*ys).astype(o_ref.dtype)


def _matmul(x, ws, combine, out_dtype, n_out, tn, name, tm=MM_ROWS):
    m, k = x.shape
    n_w = len(ws)
    return pl.pallas_call(
        functools.partial(_mm_kernel, n_w, combine),
        grid=(n_out // tn, m // tm),
        in_specs=[pl.BlockSpec((tm, k), lambda j, i: (i, 0))]
                 + [pl.BlockSpec((k, tn), lambda j, i: (0, j))] * n_w,
        out_specs=pl.BlockSpec((tm, tn), lambda j, i: (i, j)),
        out_shape=jax.ShapeDtypeStruct((m, n_out), out_dtype),
        scratch_shapes=[pltpu.VMEM((k, tn), BF16)] * n_w,
        compiler_params=_params("arbitrary", "arbitrary"),
        name=name,
    )(x, *ws)


def _swiglu_combine(g, u):
    return (g * jax.nn.sigmoid(g)) * u


def _down_kernel(a_ref, w_ref, o_ref):
    o_ref[...] = jnp.dot(a_ref[...], w_ref[...].astype(BF16), preferred_element_type=F32)


def _down(a, w):
    m, f = a.shape
    d = w.shape[1]
    tn = MXU_COLS
    return pl.pallas_call(
        _down_kernel,
        grid=(m // MM_ROWS, d // tn),
        in_specs=[pl.BlockSpec((MM_ROWS, f), lambda i, j: (i, 0), pipeline_mode=pl.Buffered(1)),
                  pl.BlockSpec((f, tn), lambda i, j: (0, j))],
        out_specs=pl.BlockSpec((MM_ROWS, tn), lambda i, j: (i, j)),
        out_shape=jax.ShapeDtypeStruct((m, d), F32),
        compiler_params=_params("parallel", "arbitrary"),
        name="ffn_down",
    )(a, w)


def _softmax_with_sink(s, sink):
    m = jnp.maximum(jnp.max(s, axis=-1, keepdims=True), sink)
    p = jnp.exp(s - m)
    denom = jnp.sum(p, axis=-1, keepdims=True) + jnp.exp(sink - m)
    return p / denom


def _qk(q, k):
    return lax.dot_general(q, k, (((1,), (1,)), ((), ())), preferred_element_type=F32) * ATTN_SCALE


def _ctx_attn_kernel(sink_ref, q_ref, k_ref, v_ref, o_ref):
    h = pl.program_id(1)
    k = k_ref[...].astype(BF16)
    v = v_ref[...].astype(BF16)
    for g in range(GQA_GROUP):
        cols = slice(g * HEAD_DIM, (g + 1) * HEAD_DIM)
        p = _softmax_with_sink(_qk(q_ref[:, cols].astype(BF16), k), sink_ref[h * GQA_GROUP + g])
        o_ref[:, cols] = jnp.dot(p.astype(BF16), v, preferred_element_type=F32)


def _ctx_attention(z, sink, n_batch, seq):
    qw = GQA_GROUP * HEAD_DIM
    k_col = N_Q_HEADS * HEAD_DIM // HEAD_DIM
    v_col = k_col + N_KV_HEADS
    return pl.pallas_call(
        _ctx_attn_kernel,
        grid=(n_batch, N_KV_HEADS),
        in_specs=[pl.BlockSpec(memory_space=pltpu.SMEM),
                  pl.BlockSpec((seq, qw), lambda b, h: (b, h)),
                  pl.BlockSpec((seq, HEAD_DIM), lambda b, h: (b, k_col + h)),
                  pl.BlockSpec((seq, HEAD_DIM), lambda b, h: (b, v_col + h))],
        out_specs=pl.BlockSpec((seq, qw), lambda b, h: (b, h)),
        out_shape=jax.ShapeDtypeStruct((n_batch * seq, N_Q_HEADS * HEAD_DIM), F32),
        compiler_params=_params("parallel", "parallel"),
        name="ctx_attention",
    )(sink, z, z, z)


def _rope(x, cos, sin_signed):
    lane = lax.broadcasted_iota(jnp.int32, x.shape, 1)
    first = (lane % ROPE_AXIS_DIM) < ROPE_PAIRS
    partner = jnp.where(first, pltpu.roll(x, HEAD_DIM - ROPE_PAIRS, axis=1),
                        pltpu.roll(x, ROPE_PAIRS, axis=1))
    return x * cos + partner * sin_signed


def _lat_attn_kernel(sink_ref, q_ref, k_ref, v_ref, kc_ref, vc_ref, cos_ref, sin_ref, o_ref,
                     kr_ref, vb_ref):
    h = pl.program_id(1)
    i = pl.program_id(2)
    n_blk = pl.num_programs(2)
    blk = q_ref.shape[0]

    @pl.when(i == 0)
    def _():
        kr_ref[...] = _rope(k_ref[...], cos_ref[...], sin_ref[...]).astype(BF16)
        vb_ref[...] = v_ref[...].astype(BF16)

    q0 = pl.multiple_of(i * blk, blk)
    w0 = pl.multiple_of(jnp.clip(i - 1, 0, n_blk - 3) * blk, blk)
    kw = kr_ref[pl.ds(w0, 3 * blk), :]
    vw = vb_ref[pl.ds(w0, 3 * blk), :]
    kc = kc_ref[...].astype(BF16)
    vc = vc_ref[...].astype(BF16)
    cos = cos_ref[pl.ds(q0, blk), :]
    sin = sin_ref[pl.ds(q0, blk), :]
    q_pos = q0 + lax.broadcasted_iota(jnp.int32, (blk, 3 * blk), 0)
    k_pos = w0 + lax.broadcasted_iota(jnp.int32, (blk, 3 * blk), 1)
    in_window = jnp.abs(k_pos - q_pos) <= WINDOW
    for g in range(GQA_GROUP):
        cols = slice(g * HEAD_DIM, (g + 1) * HEAD_DIM)
        q = _rope(q_ref[:, cols], cos, sin).astype(BF16)
        s = jnp.concatenate([jnp.where(in_window, _qk(q, kw), NEG), _qk(q, kc)], axis=-1)
        p = _softmax_with_sink(s, sink_ref[h * GQA_GROUP + g]).astype(BF16)
        o_ref[:, cols] = (jnp.dot(p[:, :3 * blk], vw, preferred_element_type=F32)
                          + jnp.dot(p[:, 3 * blk:], vc, preferred_element_type=F32))


def _lat_attention(z, cache_k, cache_v, sink, cos, sin_signed, row0, n_batch, seq):
    blk = WINDOW
    n_blk = seq // blk
    past = cache_k.shape[1]
    qw = GQA_GROUP * HEAD_DIM
    k_col = N_Q_HEADS
    v_col = k_col + N_KV_HEADS
    return pl.pallas_call(
        _lat_attn_kernel,
        grid=(n_batch, N_KV_HEADS, n_blk),
        in_specs=[pl.BlockSpec(memory_space=pltpu.SMEM),
                  pl.BlockSpec((blk, qw), lambda b, h, i: (row0 // blk + b * n_blk + i, h)),
                  pl.BlockSpec((seq, HEAD_DIM), lambda b, h, i: (row0 // seq + b, k_col + h)),
                  pl.BlockSpec((seq, HEAD_DIM), lambda b, h, i: (row0 // seq + b, v_col + h)),
                  pl.BlockSpec((None, past, HEAD_DIM), lambda b, h, i: (b, 0, h)),
                  pl.BlockSpec((None, past, HEAD_DIM), lambda b, h, i: (b, 0, h)),
                  pl.BlockSpec((seq, HEAD_DIM), lambda b, h, i: (0, 0)),
                  pl.BlockSpec((seq, HEAD_DIM), lambda b, h, i: (0, 0))],
        out_specs=pl.BlockSpec((blk, qw), lambda b, h, i: (b * n_blk + i, h)),
        out_shape=jax.ShapeDtypeStruct((n_batch * seq, N_Q_HEADS * HEAD_DIM), F32),
        scratch_shapes=[pltpu.VMEM((seq, HEAD_DIM), BF16), pltpu.VMEM((seq, HEAD_DIM), BF16)],
        compiler_params=_params("parallel", "parallel", "arbitrary"),
        name="latent_attention",
    )(sink, z, z, z, cache_k, cache_v, cos, sin_signed)


def _rope_tables(seq):
    t = jnp.arange(seq)
    pos = jnp.stack([t // GRID_W, t % GRID_W], axis=1).astype(F32)
    inv_freq = ROPE_THETA ** (-jnp.arange(ROPE_PAIRS, dtype=F32) / ROPE_PAIRS)
    ang = pos[:, :, None] * inv_freq[None, None, :]
    cos, sin = jnp.cos(ang), jnp.sin(ang)
    cos = jnp.concatenate([cos, cos], axis=-1).reshape(seq, HEAD_DIM)
    sin = jnp.concatenate([-sin, sin], axis=-1).reshape(seq, HEAD_DIM)
    return cos, sin


def _conv_kernel(n_ctx, lat_seq, gb_ref, gc_ref, hc_ref, gcp_ref, hcp_ref, gcn_ref, hcn_ref,
                 w_ref, o_ref):
    t = pl.program_id(0)
    rows = gb_ref.shape[0]
    r0 = t * rows
    is_first = jnp.logical_or(r0 < n_ctx, (r0 - n_ctx) % lat_seq == 0)
    is_last = jnp.logical_or(r0 < n_ctx, (r0 + rows - n_ctx) % lat_seq == 0)
    u = gc_ref[...] * hc_ref[...]
    halo_rows = gcp_ref.shape[0]
    u_before = jnp.where(is_first, 0.0, gcp_ref[pl.ds(halo_rows - 1, 1), :] * hcp_ref[pl.ds(halo_rows - 1, 1), :])
    u_after = jnp.where(is_last, 0.0, gcn_ref[pl.ds(0, 1), :] * hcn_ref[pl.ds(0, 1), :])
    row = lax.broadcasted_iota(jnp.int32, u.shape, 0)
    u_prev = jnp.where(row == 0, u_before, pltpu.roll(u, 1, axis=0))
    u_next = jnp.where(row == rows - 1, u_after, pltpu.roll(u, rows - 1, axis=0))
    y = u_prev * w_ref[pl.ds(0, 1), :] + u * w_ref[pl.ds(1, 1), :] + u_next * w_ref[pl.ds(2, 1), :]
    o_ref[...] = gb_ref[...] * y


def _short_conv(z, w_conv, col0, width, n_ctx, ctx_seq, lat_seq):
    assert ctx_seq == ROW_TILE
    n = z.shape[0]
    tc = 1024
    halo = 8
    nc = width // tc
    b0, c0, h0 = col0 // tc, (col0 + width) // tc, (col0 + 2 * width) // tc
    per = ROW_TILE // halo
    last_halo = n // halo - 1
    prev_map = lambda off: (lambda t, c: (jnp.maximum(t * per - 1, 0), off + c))
    next_map = lambda off: (lambda t, c: (jnp.minimum((t + 1) * per, last_halo), off + c))
    return pl.pallas_call(
        functools.partial(_conv_kernel, n_ctx, lat_seq),
        grid=(n // ROW_TILE, nc),
        in_specs=[pl.BlockSpec((ROW_TILE, tc), lambda t, c: (t, b0 + c)),
                  pl.BlockSpec((ROW_TILE, tc), lambda t, c: (t, c0 + c)),
                  pl.BlockSpec((ROW_TILE, tc), lambda t, c: (t, h0 + c)),
                  pl.BlockSpec((halo, tc), prev_map(c0)),
                  pl.BlockSpec((halo, tc), prev_map(h0)),
                  pl.BlockSpec((halo, tc), next_map(c0)),
                  pl.BlockSpec((halo, tc), next_map(h0)),
                  pl.BlockSpec((w_conv.shape[0], tc), lambda t, c: (0, c))],
        out_specs=pl.BlockSpec((ROW_TILE, tc), lambda t, c: (t, c)),
        out_shape=jax.ShapeDtypeStruct((n, width), F32),
        compiler_params=_params("parallel", "parallel"),
        name="short_conv",
    )(z, z, z, z, z, z, z, w_conv)


def _merge_kernel(a_ref, c_ref, ga_ref, gc_ref, o_ref):
    wa = a_ref.shape[1]
    o_ref[:, :wa] = _rms(a_ref[...], ga_ref[...]).astype(BF16)
    o_ref[:, wa:] = _rms(c_ref[...], gc_ref[...]).astype(BF16)


def _merge(attn_ctx, attn_lat, conv_o, g_a, g_c):
    n, wc = conv_o.shape
    wa = attn_ctx.shape[1]
    t_ctx = attn_ctx.shape[0] // ROW_TILE
    t_lat = attn_lat.shape[0] // ROW_TILE

    def kern(ac_ref, al_ref, c_ref, ga_ref, gc_ref, o_ref):
        is_ctx = pl.program_id(0) < t_ctx

        @pl.when(is_ctx)
        def _():
            _merge_kernel(ac_ref, c_ref, ga_ref, gc_ref, o_ref)

        @pl.when(jnp.logical_not(is_ctx))
        def _():
            _merge_kernel(al_ref, c_ref, ga_ref, gc_ref, o_ref)

    return pl.pallas_call(
        kern,
        grid=(n // ROW_TILE,),
        in_specs=[pl.BlockSpec((ROW_TILE, wa), lambda i: (jnp.minimum(i, t_ctx - 1), 0)),
                  pl.BlockSpec((ROW_TILE, wa), lambda i: (jnp.clip(i - t_ctx, 0, t_lat - 1), 0)),
                  pl.BlockSpec((ROW_TILE, wc), lambda i: (i, 0)),
                  pl.BlockSpec((1, wa), lambda i: (0, 0)),
                  pl.BlockSpec((1, wc), lambda i: (0, 0))],
        out_specs=pl.BlockSpec((ROW_TILE, wa + wc), lambda i: (i, 0)),
        out_shape=jax.ShapeDtypeStruct((n, wa + wc), BF16),
        compiler_params=_params("parallel"),
        name="merge_norm",
    )(attn_ctx, attn_lat, conv_o, g_a.reshape(1, wa), g_c.reshape(1, wc))


def kernel(x_prompt, x_sample, c, cache_k, cache_v, c_ctx, w_mod, b_mod, g_pre, g_post, w_in, w_conv,
           sink, g_attn_out, g_conv_out, w_o, w_ffn1_gate, w_ffn1_up, w_ffn1_down,
           w_ffn2_gate, w_ffn2_up, w_ffn2_down):
    batch, seq, d = x_prompt.shape
    dec_batch, dec_seq, _ = x_sample.shape
    depth = w_mod.shape[0]
    past = cache_k.shape[2]
    n_ctx = batch * seq
    n_lat = dec_batch * dec_seq
    attn_w = N_Q_HEADS * HEAD_DIM
    kv_w = N_KV_HEADS * HEAD_DIM
    conv_w = d - attn_w
    d_ff = w_ffn1_gate.shape[2]

    hs = (x_prompt.reshape(n_ctx, d), x_sample.reshape(n_lat, d))
    cond = jnp.concatenate([c_ctx[None, :], c, jnp.zeros((8 - 1 - dec_batch, d), F32)], axis=0)
    cos, sin_signed = _rope_tables(dec_seq)
    post = functools.partial(_post, n_ctx=n_ctx, lat_seq=dec_seq)
    wide = 2 * MXU_COLS

    ks_new, vs_new = [], []
    for l in range(depth):
        mod3 = _modulation(cond, w_mod[l], b_mod[l]).reshape(8, 3 * N_SUB, d)
        u = _pre(0, hs, mod3, g_pre[l], n_ctx, dec_seq)

        a = _matmul(u, [w_ffn1_gate[l], w_ffn1_up[l]], _swiglu_combine, BF16, d_ff, MXU_COLS, "ffn_up")
        o = _down(a, w_ffn1_down[l])
        h, u = post(0, 0.5, hs, o, mod3, g_post[l], g_pre[l], with_pre=True)

        z = _matmul(u, [w_in[l]], lambda y: y, F32, w_in.shape[2], wide, "in_proj")
        attn_ctx = _ctx_attention(z, sink[l], batch, seq)
        attn_lat = _lat_attention(z, cache_k[:, l].reshape(dec_batch, past, kv_w),
                                  cache_v[:, l].reshape(dec_batch, past, kv_w),
                                  sink[l], cos, sin_signed, n_ctx, dec_batch, dec_seq)
        conv_o = _short_conv(z, w_conv[l], attn_w + 2 * kv_w, conv_w, n_ctx, seq, dec_seq)
        mrg = _merge(attn_ctx, attn_lat, conv_o, g_attn_out[l], g_conv_out[l])
        o = _matmul(mrg, [w_o[l]], lambda y: y, F32, d, wide, "out_proj")
        h, u = post(1, 1.0, (h,), o, mod3, g_post[l], g_pre[l], with_pre=True)

        a = _matmul(u, [w_ffn2_gate[l], w_ffn2_up[l]], _swiglu_combine, BF16, d_ff, MXU_COLS, "ffn_up")
        o = _down(a, w_ffn2_down[l])
        if l + 1 < depth:
            (h,) = post(2, 0.5, (h,), o, mod3, g_post[l], g_pre[l], with_pre=False)
            hs = (h,)
        ks_new.append(z[:n_ctx, attn_w:attn_w + kv_w].reshape(batch, seq, N_KV_HEADS, HEAD_DIM))
        vs_new.append(z[:n_ctx, attn_w + kv_w:attn_w + 2 * kv_w].reshape(batch, seq, N_KV_HEADS, HEAD_DIM))

    (y_prompt,) = post(2, 0.5, (h,), o, mod3, g_post[depth - 1], g_pre[depth - 1], with_pre=False,
                       row0=0, rows=n_ctx)
    (y_sample,) = post(2, 0.5, (h,), o, mod3, g_post[depth - 1], g_pre[depth - 1], with_pre=False,
                       row0=n_ctx, rows=n_lat)
    return (y_prompt.reshape(batch, seq, d), y_sample.reshape(dec_batch, dec_seq, d),
            jnp.stack(ks_new, axis=1), jnp.stack(vs_new, axis=1))
```

```python
import functools

import jax
import jax.numpy as jnp
from jax import lax
from jax.experimental import pallas as pl
from jax.experimental.pallas import tpu as pltpu

F32 = jnp.float32
BF16 = jnp.bfloat16

N_Q_HEADS = 16
N_KV_HEADS = 4
GQA_GROUP = N_Q_HEADS // N_KV_HEADS
HEAD_DIM = 128
GRID_W = 64
WINDOW = 128
ROPE_THETA = 10000.0
ROPE_AXIS_DIM = HEAD_DIM // 2
ROPE_PAIRS = ROPE_AXIS_DIM // 2
N_SUB = 3
EPS = 1e-6
NEG = -1e30
ATTN_SCALE = HEAD_DIM ** -0.5

VMEM_LIMIT_BYTES = 60 * 1024 * 1024
MXU_COLS = 256
ROW_TILE = 256
MM_ROWS = 1024


def _params(*sem):
    return pltpu.CompilerParams(dimension_semantics=sem, vmem_limit_bytes=VMEM_LIMIT_BYTES)


def _rms(x, g):
    return x * lax.rsqrt(jnp.mean(x * x, axis=-1, keepdims=True) + EPS) * g


def _mod_kernel(c_ref, w_ref, b_ref, o_ref):
    x = c_ref[...]
    x = (x * jax.nn.sigmoid(x)).astype(BF16)
    o_ref[...] = jnp.dot(x, w_ref[...].astype(BF16), preferred_element_type=F32) + b_ref[...]


def _modulation(cond, w_mod, b_mod):
    rows, d = cond.shape
    n = w_mod.shape[1]
    tn = 512
    return pl.pallas_call(
        _mod_kernel,
        grid=(n // tn,),
        in_specs=[pl.BlockSpec((rows, d), lambda j: (0, 0)),
                  pl.BlockSpec((d, tn), lambda j: (0, j)),
                  pl.BlockSpec((1, tn), lambda j: (0, j))],
        out_specs=pl.BlockSpec((rows, tn), lambda j: (0, j)),
        out_shape=jax.ShapeDtypeStruct((rows, n), F32),
        compiler_params=_params("arbitrary"),
        name="modulation",
    )(cond, w_mod, b_mod.reshape(1, n))


def _load_rows(h_refs, n_ctx_tiles):
    if len(h_refs) == 1:
        return h_refs[0][...]
    return jnp.where(pl.program_id(0) < n_ctx_tiles, h_refs[0][...], h_refs[1][...])


def _pre_kernel(s, n_h, n_ctx_tiles, *refs):
    h_refs, (m_ref, gpre_ref, u_ref) = refs[:n_h], refs[n_h:]
    shift = m_ref[pl.ds(3 * s, 1), :]
    scale = m_ref[pl.ds(3 * s + 1, 1), :]
    u = _rms(_load_rows(h_refs, n_ctx_tiles), gpre_ref[pl.ds(s, 1), :]) * (1.0 + scale) + shift
    u_ref[...] = u.astype(u_ref.dtype)


def _post_kernel(s, res_w, with_pre, n_h, n_ctx_tiles, *refs):
    h_refs, (o_ref, m_ref, gpost_ref, gpre_ref, hn_ref, *u_ref) = refs[:n_h], refs[n_h:]
    gate = m_ref[pl.ds(3 * s + 2, 1), :]
    hn = _load_rows(h_refs, n_ctx_tiles) + (res_w * gate) * _rms(o_ref[...], gpost_ref[pl.ds(s, 1), :])
    hn_ref[...] = hn
    if with_pre:
        shift = m_ref[pl.ds(3 * (s + 1), 1), :]
        scale = m_ref[pl.ds(3 * (s + 1) + 1, 1), :]
        u = _rms(hn, gpre_ref[pl.ds(s + 1, 1), :]) * (1.0 + scale) + shift
        u_ref[0][...] = u.astype(BF16)


def _group_of_tile(i, tile_rows, n_ctx, lat_seq):
    r0 = i * tile_rows
    return jnp.where(r0 < n_ctx, 0, 1 + (r0 - n_ctx) // lat_seq)


def _row_specs(hs, d, t0=0):
    if len(hs) == 1:
        return [pl.BlockSpec((ROW_TILE, d), lambda i: (i + t0, 0))], 0
    t_ctx, t_lat = (x.shape[0] // ROW_TILE for x in hs)
    return [pl.BlockSpec((ROW_TILE, d), lambda i: (jnp.minimum(i, t_ctx - 1), 0)),
            pl.BlockSpec((ROW_TILE, d), lambda i: (jnp.clip(i - t_ctx, 0, t_lat - 1), 0))], t_ctx


def _pre(s, hs, mod3, g_pre, n_ctx, lat_seq):
    n = sum(x.shape[0] for x in hs)
    d = hs[0].shape[1]
    grp = functools.partial(_group_of_tile, tile_rows=ROW_TILE, n_ctx=n_ctx, lat_seq=lat_seq)
    h_specs, t_ctx = _row_specs(hs, d)
    return pl.pallas_call(
        functools.partial(_pre_kernel, s, len(hs), t_ctx),
        grid=(n // ROW_TILE,),
        in_specs=h_specs + [pl.BlockSpec((None, 3 * N_SUB, d), lambda i: (grp(i), 0, 0)),
                            pl.BlockSpec((N_SUB, d), lambda i: (0, 0))],
        out_specs=pl.BlockSpec((ROW_TILE, d), lambda i: (i, 0)),
        out_shape=jax.ShapeDtypeStruct((n, d), BF16),
        compiler_params=_params("parallel"),
        name=f"pre{s}",
    )(*hs, mod3, g_pre)


def _post(s, res_w, hs, o, mod3, g_post, g_pre, n_ctx, lat_seq, with_pre, row0=0, rows=None):
    d = o.shape[1]
    rows = o.shape[0] if rows is None else rows
    t0 = row0 // ROW_TILE
    grp = lambda i: _group_of_tile(i + t0, ROW_TILE, n_ctx, lat_seq)
    h_specs, t_ctx = _row_specs(hs, d, t0)
    out_shape = [jax.ShapeDtypeStruct((rows, d), F32)]
    out_specs = [pl.BlockSpec((ROW_TILE, d), lambda i: (i, 0))]
    if with_pre:
        out_shape.append(jax.ShapeDtypeStruct((rows, d), BF16))
        out_specs.append(pl.BlockSpec((ROW_TILE, d), lambda i: (i, 0)))
    return pl.pallas_call(
        functools.partial(_post_kernel, s, res_w, with_pre, len(hs), t_ctx),
        grid=(rows // ROW_TILE,),
        in_specs=h_specs + [pl.BlockSpec((ROW_TILE, d), lambda i: (i + t0, 0)),
                            pl.BlockSpec((None, 3 * N_SUB, d), lambda i: (grp(i), 0, 0)),
                            pl.BlockSpec((N_SUB, d), lambda i: (0, 0)),
                            pl.BlockSpec((N_SUB, d), lambda i: (0, 0))],
        out_specs=out_specs,
        out_shape=out_shape,
        compiler_params=_params("parallel"),
        name=f"post{s}",
    )(*hs, o, mod3, g_post, g_pre)


def _mm_kernel(n_w, n_out, combine, x_ref, *refs):
    w_refs, o_refs, wb_refs = refs[:n_w], refs[n_w:n_w + n_out], refs[n_w + n_out:]

    @pl.when(pl.program_id(1) == 0)
    def _():
        for w_ref, wb_ref in zip(w_refs, wb_refs):
            wb_ref[...] = w_ref[...].astype(BF16)

    x = x_ref[...]
    ys = [jnp.dot(x, wb_ref[...], preferred_element_type=F32) for wb_ref in wb_refs]
    for o_ref, y in zip(o_refs, combine(*ys)):
        o_ref[...] = y.astype(o_ref.dtype)


def _matmul(x, ws, combine, out_dtypes, n_cols, tn, name):
    m, k = x.shape
    w_specs = [pl.BlockSpec((k, tn), lambda j, i, c0=col0 // tn: (0, j + c0)) for _, col0 in ws]
    return pl.pallas_call(
        functools.partial(_mm_kernel, len(ws), len(out_dtypes), combine),
        grid=(n_cols // tn, m // MM_ROWS),
        in_specs=[pl.BlockSpec((MM_ROWS, k), lambda j, i: (i, 0))] + w_specs,
        out_specs=[pl.BlockSpec((MM_ROWS, tn), lambda j, i: (i, j))] * len(out_dtypes),
        out_shape=[jax.ShapeDtypeStruct((m, n_cols), dt) for dt in out_dtypes],
        scratch_shapes=[pltpu.VMEM((k, tn), BF16)] * len(ws),
        compiler_params=_params("arbitrary", "arbitrary"),
        name=name,
    )(x, *(w for w, _ in ws))


def _swiglu_combine(g, u):
    return ((g * jax.nn.sigmoid(g)) * u,)


def _identity_combine(y):
    return (y,)


def _conv_gate_combine(gb, gc, hc):
    return gb, gc * hc


def _down_kernel(a_ref, w_ref, o_ref):
    o_ref[...] = jnp.dot(a_ref[...], w_ref[...].astype(BF16), preferred_element_type=F32)


def _down(a, w):
    m, f = a.shape
    d = w.shape[1]
    tn = MXU_COLS
    return pl.pallas_call(
        _down_kernel,
        grid=(m // MM_ROWS, d // tn),
        in_specs=[pl.BlockSpec((MM_ROWS, f), lambda i, j: (i, 0), pipeline_mode=pl.Buffered(1)),
                  pl.BlockSpec((f, tn), lambda i, j: (0, j))],
        out_specs=pl.BlockSpec((MM_ROWS, tn), lambda i, j: (i, j)),
        out_shape=jax.ShapeDtypeStruct((m, d), F32),
        compiler_params=_params("parallel", "arbitrary"),
        name="ffn_down",
    )(a, w)


def _softmax_with_sink(s, sink):
    m = jnp.maximum(jnp.max(s, axis=-1, keepdims=True), sink)
    p = jnp.exp(s - m)
    denom = jnp.sum(p, axis=-1, keepdims=True) + jnp.exp(sink - m)
    return p * (1.0 / denom)


def _qk(q, k):
    return lax.dot_general(q, k, (((1,), (1,)), ((), ())), preferred_element_type=F32) * ATTN_SCALE


def _group_sinks(sink_ref, h, rows_per_head):
    head = lax.broadcasted_iota(jnp.int32, (GQA_GROUP * rows_per_head, 1), 0) // rows_per_head
    col = jnp.full(head.shape, sink_ref[h * GQA_GROUP], F32)
    for g in range(1, GQA_GROUP):
        col = jnp.where(head == g, sink_ref[h * GQA_GROUP + g], col)
    return col


def _stack_heads(q_ref, prep):
    return jnp.concatenate([prep(q_ref[:, g * HEAD_DIM:(g + 1) * HEAD_DIM]).astype(BF16)
                            for g in range(GQA_GROUP)], axis=0)


def _unstack_heads(o, o_ref):
    rows = o_ref.shape[0]
    for g in range(GQA_GROUP):
        o_ref[:, g * HEAD_DIM:(g + 1) * HEAD_DIM] = o[g * rows:(g + 1) * rows, :]


def _ctx_attn_kernel(sink_ref, q_ref, k_ref, v_ref, o_ref):
    h = pl.program_id(1)
    q = _stack_heads(q_ref, lambda x: x)
    s = _qk(q, k_ref[...].astype(BF16))
    p = _softmax_with_sink(s, _group_sinks(sink_ref, h, q_ref.shape[0]))
    _unstack_heads(jnp.dot(p.astype(BF16), v_ref[...].astype(BF16), preferred_element_type=F32), o_ref)


def _ctx_attention(z, sink, n_batch, seq):
    qw = GQA_GROUP * HEAD_DIM
    k_col = N_Q_HEADS * HEAD_DIM // HEAD_DIM
    v_col = k_col + N_KV_HEADS
    return pl.pallas_call(
        _ctx_attn_kernel,
        grid=(n_batch, N_KV_HEADS),
        in_specs=[pl.BlockSpec(memory_space=pltpu.SMEM),
                  pl.BlockSpec((seq, qw), lambda b, h: (b, h)),
                  pl.BlockSpec((seq, HEAD_DIM), lambda b, h: (b, k_col + h)),
                  pl.BlockSpec((seq, HEAD_DIM), lambda b, h: (b, v_col + h))],
        out_specs=pl.BlockSpec((seq, qw), lambda b, h: (b, h)),
        out_shape=jax.ShapeDtypeStruct((n_batch * seq, N_Q_HEADS * HEAD_DIM), F32),
        compiler_params=_params("parallel", "parallel"),
        name="ctx_attention",
    )(sink, z, z, z)


def _rope(x, cos, sin_signed):
    lane = lax.broadcasted_iota(jnp.int32, x.shape, 1)
    first = (lane % ROPE_AXIS_DIM) < ROPE_PAIRS
    partner = jnp.where(first, pltpu.roll(x, HEAD_DIM - ROPE_PAIRS, axis=1),
                        pltpu.roll(x, ROPE_PAIRS, axis=1))
    return x * cos + partner * sin_signed


def _lat_attn_kernel(sink_ref, q_ref, k_ref, v_ref, kc_ref, vc_ref, cos_ref, sin_ref, o_ref,
                     kr_ref, vb_ref, kcb_ref, vcb_ref):
    h = pl.program_id(1)
    i = pl.program_id(2)
    n_blk = pl.num_programs(2)
    blk = q_ref.shape[0]

    @pl.when(i == 0)
    def _():
        kr_ref[...] = _rope(k_ref[...], cos_ref[...], sin_ref[...]).astype(BF16)
        vb_ref[...] = v_ref[...].astype(BF16)
        kcb_ref[...] = kc_ref[...].astype(BF16)
        vcb_ref[...] = vc_ref[...].astype(BF16)

    q0 = pl.multiple_of(i * blk, blk)
    w0 = pl.multiple_of(jnp.clip(i - 1, 0, n_blk - 3) * blk, blk)
    cos = cos_ref[pl.ds(q0, blk), :]
    sin = sin_ref[pl.ds(q0, blk), :]
    q = _stack_heads(q_ref, lambda x: _rope(x, cos, sin))
    rows = GQA_GROUP * blk
    q_pos = q0 + lax.broadcasted_iota(jnp.int32, (rows, 3 * blk), 0) % blk
    k_pos = w0 + lax.broadcasted_iota(jnp.int32, (rows, 3 * blk), 1)
    in_window = jnp.abs(k_pos - q_pos) <= WINDOW
    s = jnp.concatenate([jnp.where(in_window, _qk(q, kr_ref[pl.ds(w0, 3 * blk), :]), NEG),
                         _qk(q, kcb_ref[...])], axis=-1)
    p = _softmax_with_sink(s, _group_sinks(sink_ref, h, blk)).astype(BF16)
    o = (jnp.dot(p[:, :3 * blk], vb_ref[pl.ds(w0, 3 * blk), :], preferred_element_type=F32)
         + jnp.dot(p[:, 3 * blk:], vcb_ref[...], preferred_element_type=F32))
    _unstack_heads(o, o_ref)


def _lat_attention(z, cache_k, cache_v, sink, cos, sin_signed, row0, n_batch, seq):
    blk = WINDOW
    n_blk = seq // blk
    past = cache_k.shape[1]
    qw = GQA_GROUP * HEAD_DIM
    k_col = N_Q_HEADS
    v_col = k_col + N_KV_HEADS
    return pl.pallas_call(
        _lat_attn_kernel,
        grid=(n_batch, N_KV_HEADS, n_blk),
        in_specs=[pl.BlockSpec(memory_space=pltpu.SMEM),
                  pl.BlockSpec((blk, qw), lambda b, h, i: (row0 // blk + b * n_blk + i, h)),
                  pl.BlockSpec((seq, HEAD_DIM), lambda b, h, i: (row0 // seq + b, k_col + h)),
                  pl.BlockSpec((seq, HEAD_DIM), lambda b, h, i: (row0 // seq + b, v_col + h)),
                  pl.BlockSpec((None, past, HEAD_DIM), lambda b, h, i: (b, 0, h)),
                  pl.BlockSpec((None, past, HEAD_DIM), lambda b, h, i: (b, 0, h)),
                  pl.BlockSpec((seq, HEAD_DIM), lambda b, h, i: (0, 0)),
                  pl.BlockSpec((seq, HEAD_DIM), lambda b, h, i: (0, 0))],
        out_specs=pl.BlockSpec((blk, qw), lambda b, h, i: (b * n_blk + i, h)),
        out_shape=jax.ShapeDtypeStruct((n_batch * seq, N_Q_HEADS * HEAD_DIM), F32),
        scratch_shapes=[pltpu.VMEM((seq, HEAD_DIM), BF16), pltpu.VMEM((seq, HEAD_DIM), BF16),
                        pltpu.VMEM((past, HEAD_DIM), BF16), pltpu.VMEM((past, HEAD_DIM), BF16)],
        compiler_params=_params("parallel", "parallel", "arbitrary"),
        name="latent_attention",
    )(sink, z, z, z, cache_k, cache_v, cos, sin_signed)


def _rope_tables(seq):
    t = jnp.arange(seq)
    pos = jnp.stack([t // GRID_W, t % GRID_W], axis=1).astype(F32)
    inv_freq = ROPE_THETA ** (-jnp.arange(ROPE_PAIRS, dtype=F32) / ROPE_PAIRS)
    ang = pos[:, :, None] * inv_freq[None, None, :]
    cos, sin = jnp.cos(ang), jnp.sin(ang)
    cos = jnp.concatenate([cos, cos], axis=-1).reshape(seq, HEAD_DIM)
    sin = jnp.concatenate([-sin, sin], axis=-1).reshape(seq, HEAD_DIM)
    return cos, sin


def _conv_merge_kernel(n_ctx, lat_seq, ac_ref, al_ref, gb_ref, u_ref, up_ref, un_ref, w_ref,
                       ga_ref, gc_ref, o_ref):
    t = pl.program_id(0)
    rows, wa = ac_ref.shape
    r0 = t * rows
    is_ctx = r0 < n_ctx
    is_first = jnp.logical_or(is_ctx, (r0 - n_ctx) % lat_seq == 0)
    is_last = jnp.logical_or(is_ctx, (r0 + rows - n_ctx) % lat_seq == 0)
    u = u_ref[...]
    u_before = jnp.where(is_first, 0.0, up_ref[pl.ds(up_ref.shape[0] - 1, 1), :])
    u_after = jnp.where(is_last, 0.0, un_ref[pl.ds(0, 1), :])
    row = lax.broadcasted_iota(jnp.int32, u.shape, 0)
    u_prev = jnp.where(row == 0, u_before, pltpu.roll(u, 1, axis=0))
    u_next = jnp.where(row == rows - 1, u_after, pltpu.roll(u, rows - 1, axis=0))
    y = u_prev * w_ref[pl.ds(0, 1), :] + u * w_ref[pl.ds(1, 1), :] + u_next * w_ref[pl.ds(2, 1), :]
    attn = jnp.where(is_ctx, ac_ref[...], al_ref[...])
    o_ref[:, :wa] = _rms(attn, ga_ref[...]).astype(BF16)
    o_ref[:, wa:] = _rms(gb_ref[...] * y, gc_ref[...]).astype(BF16)


def _conv_merge(attn_ctx, attn_lat, gb, u, w_conv, g_a, g_c, ctx_seq, lat_seq):
    assert ctx_seq == ROW_TILE
    n, wc = u.shape
    wa = attn_ctx.shape[1]
    n_ctx = attn_ctx.shape[0]
    t_ctx = n_ctx // ROW_TILE
    t_lat = attn_lat.shape[0] // ROW_TILE
    halo = 8
    per = ROW_TILE // halo
    last_halo = n // halo - 1
    return pl.pallas_call(
        functools.partial(_conv_merge_kernel, n_ctx, lat_seq),
        grid=(n // ROW_TILE,),
        in_specs=[pl.BlockSpec((ROW_TILE, wa), lambda t: (jnp.minimum(t, t_ctx - 1), 0)),
                  pl.BlockSpec((ROW_TILE, wa), lambda t: (jnp.clip(t - t_ctx, 0, t_lat - 1), 0)),
                  pl.BlockSpec((ROW_TILE, wc), lambda t: (t, 0)),
                  pl.BlockSpec((ROW_TILE, wc), lambda t: (t, 0)),
                  pl.BlockSpec((halo, wc), lambda t: (jnp.maximum(t * per - 1, 0), 0)),
                  pl.BlockSpec((halo, wc), lambda t: (jnp.minimum((t + 1) * per, last_halo), 0)),
                  pl.BlockSpec((w_conv.shape[0], wc), lambda t: (0, 0)),
                  pl.BlockSpec((1, wa), lambda t: (0, 0)),
                  pl.BlockSpec((1, wc), lambda t: (0, 0))],
        out_specs=pl.BlockSpec((ROW_TILE, wa + wc), lambda t: (t, 0)),
        out_shape=jax.ShapeDtypeStruct((n, wa + wc), BF16),
        compiler_params=_params("parallel"),
        name="conv_merge",
    )(attn_ctx, attn_lat, gb, u, u, u, w_conv, g_a.reshape(1, wa), g_c.reshape(1, wc))


def kernel(x_prompt, x_sample, c, cache_k, cache_v, c_ctx, w_mod, b_mod, g_pre, g_post, w_in, w_conv,
           sink, g_attn_out, g_conv_out, w_o, w_ffn1_gate, w_ffn1_up, w_ffn1_down,
           w_ffn2_gate, w_ffn2_up, w_ffn2_down):
    batch, seq, d = x_prompt.shape
    dec_batch, dec_seq, _ = x_sample.shape
    depth = w_mod.shape[0]
    past = cache_k.shape[2]
    n_ctx = batch * seq
    n_lat = dec_batch * dec_seq
    attn_w = N_Q_HEADS * HEAD_DIM
    kv_w = N_KV_HEADS * HEAD_DIM
    conv_w = d - attn_w
    d_ff = w_ffn1_gate.shape[2]

    hs = (x_prompt.reshape(n_ctx, d), x_sample.reshape(n_lat, d))
    cond = jnp.concatenate([c_ctx[None, :], c, jnp.zeros((8 - 1 - dec_batch, d), F32)], axis=0)
    cos, sin_signed = _rope_tables(dec_seq)
    post = functools.partial(_post, n_ctx=n_ctx, lat_seq=dec_seq)
    wide = 2 * MXU_COLS

    ks_new, vs_new = [], []
    for l in range(depth):
        mod3 = _modulation(cond, w_mod[l], b_mod[l]).reshape(8, 3 * N_SUB, d)
        u = _pre(0, hs, mod3, g_pre[l], n_ctx, dec_seq)

        (a,) = _matmul(u, [(w_ffn1_gate[l], 0), (w_ffn1_up[l], 0)], _swiglu_combine, [BF16], d_ff,
                       MXU_COLS, "ffn_up")
        o = _down(a, w_ffn1_down[l])
        h, u = post(0, 0.5, hs, o, mod3, g_post[l], g_pre[l], with_pre=True)

        qkv_w = attn_w + 2 * kv_w
        (zqkv,) = _matmul(u, [(w_in[l], 0)], _identity_combine, [F32], qkv_w, wide, "in_proj_qkv")
        gb, uc = _matmul(u, [(w_in[l], qkv_w), (w_in[l], qkv_w + conv_w), (w_in[l], qkv_w + 2 * conv_w)],
                         _conv_gate_combine, [F32, F32], conv_w, MXU_COLS, "in_proj_conv")
        attn_ctx = _ctx_attention(zqkv, sink[l], batch, seq)
        attn_lat = _lat_attention(zqkv, cache_k[:, l].reshape(dec_batch, past, kv_w),
                                  cache_v[:, l].reshape(dec_batch, past, kv_w),
                                  sink[l], cos, sin_signed, n_ctx, dec_batch, dec_seq)
        mrg = _conv_merge(attn_ctx, attn_lat, gb, uc, w_conv[l], g_attn_out[l], g_conv_out[l], seq, dec_seq)
        (o,) = _matmul(mrg, [(w_o[l], 0)], _identity_combine, [F32], d, wide, "out_proj")
        h, u = post(1, 1.0, (h,), o, mod3, g_post[l], g_pre[l], with_pre=True)

        (a,) = _matmul(u, [(w_ffn2_gate[l], 0), (w_ffn2_up[l], 0)], _swiglu_combine, [BF16], d_ff,
                       MXU_COLS, "ffn_up")
        o = _down(a, w_ffn2_down[l])
        if l + 1 < depth:
            (h,) = post(2, 0.5, (h,), o, mod3, g_post[l], g_pre[l], with_pre=False)
            hs = (h,)
        ks_new.append(zqkv[:n_ctx, attn_w:attn_w + kv_w].reshape(batch, seq, N_KV_HEADS, HEAD_DIM))
        vs_new.append(zqkv[:n_ctx, attn_w + kv_w:qkv_w].reshape(batch, seq, N_KV_HEADS, HEAD_DIM))

    (y_prompt,) = post(2, 0.5, (h,), o, mod3, g_post[depth - 1], g_pre[depth - 1], with_pre=False,
                       row0=0, rows=n_ctx)
    (y_sample,) = post(2, 0.5, (h,), o, mod3, g_post[depth - 1], g_pre[depth - 1], with_pre=False,
                       row0=n_ctx, rows=n_lat)
    return (y_prompt.reshape(batch, seq, d), y_sample.reshape(dec_batch, dec_seq, d),
            jnp.stack(ks_new, axis=1), jnp.stack(vs_new, axis=1))
```

```python
import functools

import jax
import jax.numpy as jnp
import numpy as np
from jax import lax
from jax.experimental import pallas as pl
from jax.experimental.pallas import tpu as pltpu

F32 = jnp.float32
BF16 = jnp.bfloat16

N_Q_HEADS = 16
N_KV_HEADS = 4
GQA_GROUP = N_Q_HEADS // N_KV_HEADS
HEAD_DIM = 128
GRID_W = 64
WINDOW = 128
ROPE_THETA = 10000.0
ROPE_AXIS_DIM = HEAD_DIM // 2
ROPE_PAIRS = ROPE_AXIS_DIM // 2
N_SUB = 3
EPS = 1e-6
NEG = -1e30
ATTN_SCALE = HEAD_DIM ** -0.5

VMEM_LIMIT_BYTES = 60 * 1024 * 1024
MXU_COLS = 256
ROW_TILE = 256
MM_ROWS = 1024


def _params(*sem):
    return pltpu.CompilerParams(dimension_semantics=sem, vmem_limit_bytes=VMEM_LIMIT_BYTES)


def _rms(x, g):
    return x * lax.rsqrt(jnp.mean(x * x, axis=-1, keepdims=True) + EPS) * g


def _mod_tile(c_ref, w_ref, b_ref, o_ref):
    x = c_ref[...]
    x = (x * jax.nn.sigmoid(x)).astype(BF16)
    o_ref[...] = jnp.dot(x, w_ref[...].astype(BF16), preferred_element_type=F32) + b_ref[...]


def _modulation(cond, w_mod, b_mod2, n_cols):
    rows, d = cond.shape
    tn = 512
    return pl.pallas_call(
        _mod_tile,
        grid=(n_cols // tn,),
        in_specs=[pl.BlockSpec((rows, d), lambda j: (0, 0)),
                  pl.BlockSpec((d, tn), lambda j: (0, j)),
                  pl.BlockSpec((1, tn), lambda j: (0, j))],
        out_specs=pl.BlockSpec((rows, tn), lambda j: (0, j)),
        out_shape=jax.ShapeDtypeStruct((rows, n_cols), F32),
        compiler_params=_params("arbitrary"),
        name="modulation",
    )(cond, w_mod, b_mod2)


def _load_rows(h_refs, n_ctx_tiles):
    if len(h_refs) == 1:
        return h_refs[0][...]
    return jnp.where(pl.program_id(0) < n_ctx_tiles, h_refs[0][...], h_refs[1][...])


def _pre_kernel(s, n_h, n_ctx_tiles, *refs):
    h_refs, (m_ref, gpre_ref, u_ref) = refs[:n_h], refs[n_h:]
    shift = m_ref[pl.ds(3 * s, 1), :]
    scale = m_ref[pl.ds(3 * s + 1, 1), :]
    u = _rms(_load_rows(h_refs, n_ctx_tiles), gpre_ref[pl.ds(s, 1), :]) * (1.0 + scale) + shift
    u_ref[...] = u.astype(u_ref.dtype)


def _post_kernel(s, res_w, with_pre, n_h, n_ctx_tiles, *refs):
    h_refs, (o_ref, m_ref, gpost_ref, gpre_ref, hn_ref, *u_ref) = refs[:n_h], refs[n_h:]
    gate = m_ref[pl.ds(3 * s + 2, 1), :]
    hn = _load_rows(h_refs, n_ctx_tiles) + (res_w * gate) * _rms(o_ref[...], gpost_ref[pl.ds(s, 1), :])
    hn_ref[...] = hn
    if with_pre:
        shift = m_ref[pl.ds(3 * (s + 1), 1), :]
        scale = m_ref[pl.ds(3 * (s + 1) + 1, 1), :]
        u = _rms(hn, gpre_ref[pl.ds(s + 1, 1), :]) * (1.0 + scale) + shift
        u_ref[0][...] = u.astype(BF16)


def _group_of_tile(i, tile_rows, n_ctx, lat_seq):
    r0 = i * tile_rows
    return jnp.where(r0 < n_ctx, 0, 1 + (r0 - n_ctx) // lat_seq)


def _row_specs(hs, d, t0=0):
    if len(hs) == 1:
        return [pl.BlockSpec((ROW_TILE, d), lambda i: (i + t0, 0))], 0
    t_ctx, t_lat = (x.shape[0] // ROW_TILE for x in hs)
    return [pl.BlockSpec((ROW_TILE, d), lambda i: (jnp.minimum(i, t_ctx - 1), 0)),
            pl.BlockSpec((ROW_TILE, d), lambda i: (jnp.clip(i - t_ctx, 0, t_lat - 1), 0))], t_ctx


def _pre(s, hs, mod3, g_pre, n_ctx, lat_seq):
    n = sum(x.shape[0] for x in hs)
    d = hs[0].shape[1]
    grp = functools.partial(_group_of_tile, tile_rows=ROW_TILE, n_ctx=n_ctx, lat_seq=lat_seq)
    h_specs, t_ctx = _row_specs(hs, d)
    return pl.pallas_call(
        functools.partial(_pre_kernel, s, len(hs), t_ctx),
        grid=(n // ROW_TILE,),
        in_specs=h_specs + [pl.BlockSpec((None, mod3.shape[1], d), lambda i: (grp(i), 0, 0)),
                            pl.BlockSpec((N_SUB, d), lambda i: (0, 0))],
        out_specs=pl.BlockSpec((ROW_TILE, d), lambda i: (i, 0)),
        out_shape=jax.ShapeDtypeStruct((n, d), BF16),
        compiler_params=_params("parallel"),
        name=f"pre{s}",
    )(*hs, mod3, g_pre)


def _post(s, res_w, hs, o, mod3, g_post, g_pre, n_ctx, lat_seq, with_pre, row0=0, rows=None):
    d = o.shape[1]
    rows = o.shape[0] if rows is None else rows
    t0 = row0 // ROW_TILE
    grp = lambda i: _group_of_tile(i + t0, ROW_TILE, n_ctx, lat_seq)
    h_specs, t_ctx = _row_specs(hs, d, t0)
    out_shape = [jax.ShapeDtypeStruct((rows, d), F32)]
    out_specs = [pl.BlockSpec((ROW_TILE, d), lambda i: (i, 0))]
    if with_pre:
        out_shape.append(jax.ShapeDtypeStruct((rows, d), BF16))
        out_specs.append(pl.BlockSpec((ROW_TILE, d), lambda i: (i, 0)))
    return pl.pallas_call(
        functools.partial(_post_kernel, s, res_w, with_pre, len(hs), t_ctx),
        grid=(rows // ROW_TILE,),
        in_specs=h_specs + [pl.BlockSpec((ROW_TILE, d), lambda i: (i + t0, 0)),
                            pl.BlockSpec((None, 3 * N_SUB, d), lambda i: (grp(i), 0, 0)),
                            pl.BlockSpec((N_SUB, d), lambda i: (0, 0)),
                            pl.BlockSpec((N_SUB, d), lambda i: (0, 0))],
        out_specs=out_specs,
        out_shape=out_shape,
        compiler_params=_params("parallel"),
        name=f"post{s}",
    )(*hs, o, mod3, g_post, g_pre)


def _mm_kernel(n_w, n_out, combine, x_ref, *refs):
    w_refs, o_refs, wb_refs = refs[:n_w], refs[n_w:n_w + n_out], refs[n_w + n_out:]

    @pl.when(pl.program_id(1) == 0)
    def _():
        for w_ref, wb_ref in zip(w_refs, wb_refs):
            wb_ref[...] = w_ref[...].astype(BF16)

    x = x_ref[...]
    ys = [jnp.dot(x, wb_ref[...], preferred_element_type=F32) for wb_ref in wb_refs]
    for o_ref, y in zip(o_refs, combine(*ys)):
        o_ref[...] = y.astype(o_ref.dtype)


def _matmul(x, ws, combine, out_dtypes, n_cols, tn, name):
    m, k = x.shape
    w_specs = [pl.BlockSpec((k, tn), lambda j, i, c0=col0 // tn: (0, j + c0)) for _, col0 in ws]
    return pl.pallas_call(
        functools.partial(_mm_kernel, len(ws), len(out_dtypes), combine),
        grid=(n_cols // tn, m // MM_ROWS),
        in_specs=[pl.BlockSpec((MM_ROWS, k), lambda j, i: (i, 0))] + w_specs,
        out_specs=[pl.BlockSpec((MM_ROWS, tn), lambda j, i: (i, j))] * len(out_dtypes),
        out_shape=[jax.ShapeDtypeStruct((m, n_cols), dt) for dt in out_dtypes],
        scratch_shapes=[pltpu.VMEM((k, tn), BF16)] * len(ws),
        compiler_params=_params("arbitrary", "arbitrary"),
        name=name,
    )(x, *(w for w, _ in ws))


def _swiglu_combine(g, u):
    return ((g * jax.nn.sigmoid(g)) * u,)


def _identity_combine(y):
    return (y,)


def _conv_gate_combine(gb, gc, hc):
    return gb, gc * hc


def _ffn_up_mod_kernel(n_side, x_ref, wg_ref, wu_ref, c_ref, wm_ref, b_ref, a_ref, m_ref, wgb_ref, wub_ref):
    _mm_kernel(2, 1, _swiglu_combine, x_ref, wg_ref, wu_ref, a_ref, wgb_ref, wub_ref)

    @pl.when(pl.program_id(0) * pl.num_programs(1) + pl.program_id(1) < n_side)
    def _():
        _mod_tile(c_ref, wm_ref, b_ref, m_ref)


def _ffn_up_with_modulation(x, wg, wu, cond, w_mod, b_mod2, col0):
    m, k = x.shape
    f = wg.shape[1]
    tn = MXU_COLS
    tmod = 128
    n_i = m // MM_ROWS
    n_side = (w_mod.shape[1] - col0) // tmod
    assert n_side <= (f // tn) * n_i
    side = lambda j, i: (0, col0 // tmod + jnp.minimum(j * n_i + i, n_side - 1))
    return pl.pallas_call(
        functools.partial(_ffn_up_mod_kernel, n_side),
        grid=(f // tn, n_i),
        in_specs=[pl.BlockSpec((MM_ROWS, k), lambda j, i: (i, 0)),
                  pl.BlockSpec((k, tn), lambda j, i: (0, j)),
                  pl.BlockSpec((k, tn), lambda j, i: (0, j)),
                  pl.BlockSpec(cond.shape, lambda j, i: (0, 0)),
                  pl.BlockSpec((w_mod.shape[0], tmod), side),
                  pl.BlockSpec((1, tmod), side)],
        out_specs=[pl.BlockSpec((MM_ROWS, tn), lambda j, i: (i, j)),
                   pl.BlockSpec((cond.shape[0], tmod), lambda j, i: (0, jnp.minimum(j * n_i + i, n_side - 1)))],
        out_shape=[jax.ShapeDtypeStruct((m, f), BF16),
                   jax.ShapeDtypeStruct((cond.shape[0], n_side * tmod), F32)],
        scratch_shapes=[pltpu.VMEM((k, tn), BF16)] * 2,
        compiler_params=_params("arbitrary", "arbitrary"),
        name="ffn_up_mod",
    )(x, wg, wu, cond, w_mod, b_mod2)


def _down_kernel(a_ref, w_ref, o_ref):
    o_ref[...] = jnp.dot(a_ref[...], w_ref[...].astype(BF16), preferred_element_type=F32)


def _down(a, w):
    m, f = a.shape
    d = w.shape[1]
    tn = MXU_COLS
    return pl.pallas_call(
        _down_kernel,
        grid=(m // MM_ROWS, d // tn),
        in_specs=[pl.BlockSpec((MM_ROWS, f), lambda i, j: (i, 0), pipeline_mode=pl.Buffered(1)),
                  pl.BlockSpec((f, tn), lambda i, j: (0, j))],
        out_specs=pl.BlockSpec((MM_ROWS, tn), lambda i, j: (i, j)),
        out_shape=jax.ShapeDtypeStruct((m, d), F32),
        compiler_params=_params("parallel", "arbitrary"),
        name="ffn_down",
    )(a, w)


def _softmax_with_sink(s, sink):
    m = jnp.maximum(jnp.max(s, axis=-1, keepdims=True), sink)
    p = jnp.exp(s - m)
    denom = jnp.sum(p, axis=-1, keepdims=True) + jnp.exp(sink - m)
    return p * (1.0 / denom)


def _qk(q, k):
    return lax.dot_general(q, k, (((1,), (1,)), ((), ())), preferred_element_type=F32) * ATTN_SCALE


def _group_sinks(sink_ref, h, rows_per_head):
    head = lax.broadcasted_iota(jnp.int32, (GQA_GROUP * rows_per_head, 1), 0) // rows_per_head
    col = jnp.full(head.shape, sink_ref[h * GQA_GROUP], F32)
    for g in range(1, GQA_GROUP):
        col = jnp.where(head == g, sink_ref[h * GQA_GROUP + g], col)
    return col


def _stack_heads(q_ref, prep):
    return jnp.concatenate([prep(q_ref[:, g * HEAD_DIM:(g + 1) * HEAD_DIM]).astype(BF16)
                            for g in range(GQA_GROUP)], axis=0)


def _unstack_heads(o, o_ref):
    rows = o_ref.shape[0]
    for g in range(GQA_GROUP):
        o_ref[:, g * HEAD_DIM:(g + 1) * HEAD_DIM] = o[g * rows:(g + 1) * rows, :]


def _ctx_attn_kernel(sink_ref, q_ref, k_ref, v_ref, o_ref, ks_ref, vs_ref):
    ks_ref[...] = k_ref[...]
    vs_ref[...] = v_ref[...]
    qw = GQA_GROUP * HEAD_DIM
    for h in range(N_KV_HEADS):
        kv_cols = slice(h * HEAD_DIM, (h + 1) * HEAD_DIM)
        q = _stack_heads(q_ref.at[:, h * qw:(h + 1) * qw], lambda x: x)
        s = _qk(q, k_ref[:, kv_cols].astype(BF16))
        p = _softmax_with_sink(s, _group_sinks(sink_ref, h, q_ref.shape[0]))
        o = jnp.dot(p.astype(BF16), v_ref[:, kv_cols].astype(BF16), preferred_element_type=F32)
        _unstack_heads(o, o_ref.at[:, h * qw:(h + 1) * qw])


def _ctx_attention(z, sink, n_batch, seq):
    q_w = N_Q_HEADS * HEAD_DIM
    kv_w = N_KV_HEADS * HEAD_DIM
    state = jax.ShapeDtypeStruct((n_batch, seq, kv_w), F32)
    return pl.pallas_call(
        _ctx_attn_kernel,
        grid=(n_batch,),
        in_specs=[pl.BlockSpec(memory_space=pltpu.SMEM),
                  pl.BlockSpec((seq, q_w), lambda b: (b, 0)),
                  pl.BlockSpec((seq, kv_w), lambda b: (b, q_w // kv_w)),
                  pl.BlockSpec((seq, kv_w), lambda b: (b, q_w // kv_w + 1))],
        out_specs=[pl.BlockSpec((seq, q_w), lambda b: (b, 0)),
                   pl.BlockSpec((None, seq, kv_w), lambda b: (b, 0, 0)),
                   pl.BlockSpec((None, seq, kv_w), lambda b: (b, 0, 0))],
        out_shape=[jax.ShapeDtypeStruct((n_batch * seq, q_w), F32), state, state],
        compiler_params=_params("parallel"),
        name="ctx_attention",
    )(sink, z, z, z)


def _rope(x, cos, sin_signed):
    lane = lax.broadcasted_iota(jnp.int32, x.shape, 1)
    first = (lane % ROPE_AXIS_DIM) < ROPE_PAIRS
    partner = jnp.where(first, pltpu.roll(x, HEAD_DIM - ROPE_PAIRS, axis=1),
                        pltpu.roll(x, ROPE_PAIRS, axis=1))
    return x * cos + partner * sin_signed


def _lat_attn_kernel(sink_ref, q_ref, k_ref, v_ref, kc_ref, vc_ref, cos_ref, sin_ref, o_ref,
                     kr_ref, vb_ref, kcb_ref, vcb_ref):
    h = pl.program_id(1)
    i = pl.program_id(2)
    n_blk = pl.num_programs(2)
    blk = q_ref.shape[0]

    @pl.when(i == 0)
    def _():
        kr_ref[...] = _rope(k_ref[...], cos_ref[...], sin_ref[...]).astype(BF16)
        vb_ref[...] = v_ref[...].astype(BF16)
        kcb_ref[...] = kc_ref[...].astype(BF16)
        vcb_ref[...] = vc_ref[...].astype(BF16)

    q0 = pl.multiple_of(i * blk, blk)
    w0 = pl.multiple_of(jnp.clip(i - 1, 0, n_blk - 3) * blk, blk)
    cos = cos_ref[pl.ds(q0, blk), :]
    sin = sin_ref[pl.ds(q0, blk), :]
    q = _stack_heads(q_ref, lambda x: _rope(x, cos, sin))
    rows = GQA_GROUP * blk
    q_pos = q0 + lax.broadcasted_iota(jnp.int32, (rows, 3 * blk), 0) % blk
    k_pos = w0 + lax.broadcasted_iota(jnp.int32, (rows, 3 * blk), 1)
    in_window = jnp.abs(k_pos - q_pos) <= WINDOW
    s = jnp.concatenate([jnp.where(in_window, _qk(q, kr_ref[pl.ds(w0, 3 * blk), :]), NEG),
                         _qk(q, kcb_ref[...])], axis=-1)
    p = _softmax_with_sink(s, _group_sinks(sink_ref, h, blk)).astype(BF16)
    o = (jnp.dot(p[:, :3 * blk], vb_ref[pl.ds(w0, 3 * blk), :], preferred_element_type=F32)
         + jnp.dot(p[:, 3 * blk:], vcb_ref[...], preferred_element_type=F32))
    _unstack_heads(o, o_ref)


def _lat_attention(z, cache_k, cache_v, sink, cos, sin_signed, row0, n_batch, seq):
    blk = WINDOW
    n_blk = seq // blk
    past = cache_k.shape[1]
    qw = GQA_GROUP * HEAD_DIM
    k_col = N_Q_HEADS
    v_col = k_col + N_KV_HEADS
    return pl.pallas_call(
        _lat_attn_kernel,
        grid=(n_batch, N_KV_HEADS, n_blk),
        in_specs=[pl.BlockSpec(memory_space=pltpu.SMEM),
                  pl.BlockSpec((blk, qw), lambda b, h, i: (row0 // blk + b * n_blk + i, h)),
                  pl.BlockSpec((seq, HEAD_DIM), lambda b, h, i: (row0 // seq + b, k_col + h)),
                  pl.BlockSpec((seq, HEAD_DIM), lambda b, h, i: (row0 // seq + b, v_col + h)),
                  pl.BlockSpec((None, past, HEAD_DIM), lambda b, h, i: (b, 0, h)),
                  pl.BlockSpec((None, past, HEAD_DIM), lambda b, h, i: (b, 0, h)),
                  pl.BlockSpec((seq, HEAD_DIM), lambda b, h, i: (0, 0)),
                  pl.BlockSpec((seq, HEAD_DIM), lambda b, h, i: (0, 0))],
        out_specs=pl.BlockSpec((blk, qw), lambda b, h, i: (b * n_blk + i, h)),
        out_shape=jax.ShapeDtypeStruct((n_batch * seq, N_Q_HEADS * HEAD_DIM), F32),
        scratch_shapes=[pltpu.VMEM((seq, HEAD_DIM), BF16), pltpu.VMEM((seq, HEAD_DIM), BF16),
                        pltpu.VMEM((past, HEAD_DIM), BF16), pltpu.VMEM((past, HEAD_DIM), BF16)],
        compiler_params=_params("parallel", "parallel", "arbitrary"),
        name="latent_attention",
    )(sink, z, z, z, cache_k, cache_v, cos, sin_signed)


def _rope_tables(seq):
    t = np.arange(seq)
    pos = np.stack([t // GRID_W, t % GRID_W], axis=1).astype(np.float64)
    inv_freq = ROPE_THETA ** (-np.arange(ROPE_PAIRS, dtype=np.float64) / ROPE_PAIRS)
    ang = pos[:, :, None] * inv_freq[None, None, :]
    cos, sin = np.cos(ang), np.sin(ang)
    cos = np.concatenate([cos, cos], axis=-1).reshape(seq, HEAD_DIM)
    sin = np.concatenate([-sin, sin], axis=-1).reshape(seq, HEAD_DIM)
    return jnp.asarray(cos, F32), jnp.asarray(sin, F32)


def _conv_merge_kernel(n_ctx, lat_seq, ac_ref, al_ref, gb_ref, u_ref, up_ref, un_ref, w_ref,
                       ga_ref, gc_ref, o_ref):
    t = pl.program_id(0)
    rows, wa = ac_ref.shape
    r0 = t * rows
    is_ctx = r0 < n_ctx
    is_first = jnp.logical_or(is_ctx, (r0 - n_ctx) % lat_seq == 0)
    is_last = jnp.logical_or(is_ctx, (r0 + rows - n_ctx) % lat_seq == 0)
    u = u_ref[...]
    u_before = jnp.where(is_first, 0.0, up_ref[pl.ds(up_ref.shape[0] - 1, 1), :])
    u_after = jnp.where(is_last, 0.0, un_ref[pl.ds(0, 1), :])
    row = lax.broadcasted_iota(jnp.int32, u.shape, 0)
    u_prev = jnp.where(row == 0, u_before, pltpu.roll(u, 1, axis=0))
    u_next = jnp.where(row == rows - 1, u_after, pltpu.roll(u, rows - 1, axis=0))
    y = u_prev * w_ref[pl.ds(0, 1), :] + u * w_ref[pl.ds(1, 1), :] + u_next * w_ref[pl.ds(2, 1), :]
    attn = jnp.where(is_ctx, ac_ref[...], al_ref[...])
    o_ref[:, :wa] = _rms(attn, ga_ref[...]).astype(BF16)
    o_ref[:, wa:] = _rms(gb_ref[...] * y, gc_ref[...]).astype(BF16)


def _conv_merge(attn_ctx, attn_lat, gb, u, w_conv, g_a, g_c, ctx_seq, lat_seq):
    assert ctx_seq == ROW_TILE
    n, wc = u.shape
    wa = attn_ctx.shape[1]
    n_ctx = attn_ctx.shape[0]
    t_ctx = n_ctx // ROW_TILE
    t_lat = attn_lat.shape[0] // ROW_TILE
    halo = 8
    per = ROW_TILE // halo
    last_halo = n // halo - 1
    return pl.pallas_call(
        functools.partial(_conv_merge_kernel, n_ctx, lat_seq),
        grid=(n // ROW_TILE,),
        in_specs=[pl.BlockSpec((ROW_TILE, wa), lambda t: (jnp.minimum(t, t_ctx - 1), 0)),
                  pl.BlockSpec((ROW_TILE, wa), lambda t: (jnp.clip(t - t_ctx, 0, t_lat - 1), 0)),
                  pl.BlockSpec((ROW_TILE, wc), lambda t: (t, 0)),
                  pl.BlockSpec((ROW_TILE, wc), lambda t: (t, 0)),
                  pl.BlockSpec((halo, wc), lambda t: (jnp.maximum(t * per - 1, 0), 0)),
                  pl.BlockSpec((halo, wc), lambda t: (jnp.minimum((t + 1) * per, last_halo), 0)),
                  pl.BlockSpec((w_conv.shape[0], wc), lambda t: (0, 0)),
                  pl.BlockSpec((1, wa), lambda t: (0, 0)),
                  pl.BlockSpec((1, wc), lambda t: (0, 0))],
        out_specs=pl.BlockSpec((ROW_TILE, wa + wc), lambda t: (t, 0)),
        out_shape=jax.ShapeDtypeStruct((n, wa + wc), BF16),
        compiler_params=_params("parallel"),
        name="conv_merge",
    )(attn_ctx, attn_lat, gb, u, u, u, w_conv, g_a.reshape(1, wa), g_c.reshape(1, wc))


def kernel(x_prompt, x_sample, c, cache_k, cache_v, c_ctx, w_mod, b_mod, g_pre, g_post, w_in, w_conv,
           sink, g_attn_out, g_conv_out, w_o, w_ffn1_gate, w_ffn1_up, w_ffn1_down,
           w_ffn2_gate, w_ffn2_up, w_ffn2_down):
    batch, seq, d = x_prompt.shape
    dec_batch, dec_seq, _ = x_sample.shape
    depth = w_mod.shape[0]
    past = cache_k.shape[2]
    n_ctx = batch * seq
    n_lat = dec_batch * dec_seq
    attn_w = N_Q_HEADS * HEAD_DIM
    kv_w = N_KV_HEADS * HEAD_DIM
    conv_w = d - attn_w
    d_ff = w_ffn1_gate.shape[2]

    hs = (x_prompt.reshape(n_ctx, d), x_sample.reshape(n_lat, d))
    cond = jnp.concatenate([c_ctx[None, :], c, jnp.zeros((8 - 1 - dec_batch, d), F32)], axis=0)
    cos, sin_signed = _rope_tables(dec_seq)
    post = functools.partial(_post, n_ctx=n_ctx, lat_seq=dec_seq)
    wide = 2 * MXU_COLS

    ks_new, vs_new = [], []
    for l in range(depth):
        b_mod2 = b_mod[l].reshape(1, -1)
        mod_head = _modulation(cond, w_mod[l], b_mod2, 2 * d)
        u = _pre(0, hs, mod_head.reshape(8, 2, d), g_pre[l], n_ctx, dec_seq)
        a, mod_tail = _ffn_up_with_modulation(u, w_ffn1_gate[l], w_ffn1_up[l], cond, w_mod[l], b_mod2, 2 * d)
        mod3 = jnp.concatenate([mod_head, mod_tail], axis=1).reshape(8, 3 * N_SUB, d)
        o = _down(a, w_ffn1_down[l])
        h, u = post(0, 0.5, hs, o, mod3, g_post[l], g_pre[l], with_pre=True)

        qkv_w = attn_w + 2 * kv_w
        (zqkv,) = _matmul(u, [(w_in[l], 0)], _identity_combine, [F32], qkv_w, wide, "in_proj_qkv")
        gb, uc = _matmul(u, [(w_in[l], qkv_w), (w_in[l], qkv_w + conv_w), (w_in[l], qkv_w + 2 * conv_w)],
                         _conv_gate_combine, [F32, F32], conv_w, MXU_COLS, "in_proj_conv")
        attn_ctx, k_new, v_new = _ctx_attention(zqkv, sink[l], batch, seq)
        attn_lat = _lat_attention(zqkv, cache_k[:, l].reshape(dec_batch, past, kv_w),
                                  cache_v[:, l].reshape(dec_batch, past, kv_w),
                                  sink[l], cos, sin_signed, n_ctx, dec_batch, dec_seq)
        mrg = _conv_merge(attn_ctx, attn_lat, gb, uc, w_conv[l], g_attn_out[l], g_conv_out[l], seq, dec_seq)
        (o,) = _matmul(mrg, [(w_o[l], 0)], _identity_combine, [F32], d, wide, "out_proj")
        h, u = post(1, 1.0, (h,), o, mod3, g_post[l], g_pre[l], with_pre=True)

        (a,) = _matmul(u, [(w_ffn2_gate[l], 0), (w_ffn2_up[l], 0)], _swiglu_combine, [BF16], d_ff,
                       MXU_COLS, "ffn_up")
        o = _down(a, w_ffn2_down[l])
        if l + 1 < depth:
            (h,) = post(2, 0.5, (h,), o, mod3, g_post[l], g_pre[l], with_pre=False)
            hs = (h,)
        ks_new.append(k_new.reshape(batch, seq, N_KV_HEADS, HEAD_DIM))
        vs_new.append(v_new.reshape(batch, seq, N_KV_HEADS, HEAD_DIM))

    (y_prompt,) = post(2, 0.5, (h,), o, mod3, g_post[depth - 1], g_pre[depth - 1], with_pre=False,
                       row0=0, rows=n_ctx)
    (y_sample,) = post(2, 0.5, (h,), o, mod3, g_post[depth - 1], g_pre[depth - 1], with_pre=False,
                       row0=n_ctx, rows=n_lat)
    return (y_prompt.reshape(batch, seq, d), y_sample.reshape(dec_batch, dec_seq, d),
            jnp.stack(ks_new, axis=1), jnp.stack(vs_new, axis=1))
```

```python
import functools

import jax
import jax.numpy as jnp
import numpy as np
from jax import lax
from jax.experimental import pallas as pl
from jax.experimental.pallas import tpu as pltpu

F32 = jnp.float32
BF16 = jnp.bfloat16

N_Q_HEADS = 16
N_KV_HEADS = 4
GQA_GROUP = N_Q_HEADS // N_KV_HEADS
HEAD_DIM = 128
GRID_W = 64
WINDOW = 128
ROPE_THETA = 10000.0
ROPE_AXIS_DIM = HEAD_DIM // 2
ROPE_PAIRS = ROPE_AXIS_DIM // 2
N_SUB = 3
EPS = 1e-6
NEG = -1e30
ATTN_SCALE = HEAD_DIM ** -0.5

VMEM_LIMIT_BYTES = 60 * 1024 * 1024
MXU_COLS = 256
ROW_TILE = 256
MM_ROWS = 1024


def _params(*sem):
    return pltpu.CompilerParams(dimension_semantics=sem, vmem_limit_bytes=VMEM_LIMIT_BYTES)


def _rms(x, g):
    return x * lax.rsqrt(jnp.mean(x * x, axis=-1, keepdims=True) + EPS) * g


def _mod_tile(c_ref, w_ref, b_ref, o_ref):
    x = c_ref[...]
    x = (x * jax.nn.sigmoid(x)).astype(BF16)
    o_ref[...] = jnp.dot(x, w_ref[...].astype(BF16), preferred_element_type=F32) + b_ref[...]


def _modulation(cond, w_mod, b_mod2, n_cols):
    rows, d = cond.shape
    tn = 512
    return pl.pallas_call(
        _mod_tile,
        grid=(n_cols // tn,),
        in_specs=[pl.BlockSpec((rows, d), lambda j: (0, 0)),
                  pl.BlockSpec((d, tn), lambda j: (0, j)),
                  pl.BlockSpec((1, tn), lambda j: (0, j))],
        out_specs=pl.BlockSpec((rows, tn), lambda j: (0, j)),
        out_shape=jax.ShapeDtypeStruct((rows, n_cols), F32),
        compiler_params=_params("arbitrary"),
        name="modulation",
    )(cond, w_mod, b_mod2)


def _load_rows(h_refs, n_ctx_tiles):
    if len(h_refs) == 1:
        return h_refs[0][...]
    return jnp.where(pl.program_id(0) < n_ctx_tiles, h_refs[0][...], h_refs[1][...])


def _pre_kernel(s, n_h, n_ctx_tiles, *refs):
    h_refs, (m_ref, gpre_ref, u_ref) = refs[:n_h], refs[n_h:]
    shift = m_ref[pl.ds(3 * s, 1), :]
    scale = m_ref[pl.ds(3 * s + 1, 1), :]
    u = _rms(_load_rows(h_refs, n_ctx_tiles), gpre_ref[pl.ds(s, 1), :]) * (1.0 + scale) + shift
    u_ref[...] = u.astype(u_ref.dtype)


def _post_kernel(s, res_w, with_pre, n_h, n_ctx_tiles, *refs):
    h_refs, (o_ref, m_ref, gpost_ref, gpre_ref, hn_ref, *u_ref) = refs[:n_h], refs[n_h:]
    gate = m_ref[pl.ds(3 * s + 2, 1), :]
    hn = _load_rows(h_refs, n_ctx_tiles) + (res_w * gate) * _rms(o_ref[...], gpost_ref[pl.ds(s, 1), :])
    hn_ref[...] = hn
    if with_pre:
        shift = m_ref[pl.ds(3 * (s + 1), 1), :]
        scale = m_ref[pl.ds(3 * (s + 1) + 1, 1), :]
        u = _rms(hn, gpre_ref[pl.ds(s + 1, 1), :]) * (1.0 + scale) + shift
        u_ref[0][...] = u.astype(BF16)


def _group_of_tile(i, tile_rows, n_ctx, lat_seq):
    r0 = i * tile_rows
    return jnp.where(r0 < n_ctx, 0, 1 + (r0 - n_ctx) // lat_seq)


def _row_specs(hs, d, t0=0):
    if len(hs) == 1:
        return [pl.BlockSpec((ROW_TILE, d), lambda i: (i + t0, 0))], 0
    t_ctx, t_lat = (x.shape[0] // ROW_TILE for x in hs)
    return [pl.BlockSpec((ROW_TILE, d), lambda i: (jnp.minimum(i, t_ctx - 1), 0)),
            pl.BlockSpec((ROW_TILE, d), lambda i: (jnp.clip(i - t_ctx, 0, t_lat - 1), 0))], t_ctx


def _pre(s, hs, mod3, g_pre, n_ctx, lat_seq):
    n = sum(x.shape[0] for x in hs)
    d = hs[0].shape[1]
    grp = functools.partial(_group_of_tile, tile_rows=ROW_TILE, n_ctx=n_ctx, lat_seq=lat_seq)
    h_specs, t_ctx = _row_specs(hs, d)
    return pl.pallas_call(
        functools.partial(_pre_kernel, s, len(hs), t_ctx),
        grid=(n // ROW_TILE,),
        in_specs=h_specs + [pl.BlockSpec((None, mod3.shape[1], d), lambda i: (grp(i), 0, 0)),
                            pl.BlockSpec((N_SUB, d), lambda i: (0, 0))],
        out_specs=pl.BlockSpec((ROW_TILE, d), lambda i: (i, 0)),
        out_shape=jax.ShapeDtypeStruct((n, d), BF16),
        compiler_params=_params("parallel"),
        name=f"pre{s}",
    )(*hs, mod3, g_pre)


def _post(s, res_w, hs, o, mod3, g_post, g_pre, n_ctx, lat_seq, with_pre, row0=0, rows=None):
    d = o.shape[1]
    rows = o.shape[0] if rows is None else rows
    t0 = row0 // ROW_TILE
    grp = lambda i: _group_of_tile(i + t0, ROW_TILE, n_ctx, lat_seq)
    h_specs, t_ctx = _row_specs(hs, d, t0)
    out_shape = [jax.ShapeDtypeStruct((rows, d), F32)]
    out_specs = [pl.BlockSpec((ROW_TILE, d), lambda i: (i, 0))]
    if with_pre:
        out_shape.append(jax.ShapeDtypeStruct((rows, d), BF16))
        out_specs.append(pl.BlockSpec((ROW_TILE, d), lambda i: (i, 0)))
    return pl.pallas_call(
        functools.partial(_post_kernel, s, res_w, with_pre, len(hs), t_ctx),
        grid=(rows // ROW_TILE,),
        in_specs=h_specs + [pl.BlockSpec((ROW_TILE, d), lambda i: (i + t0, 0)),
                            pl.BlockSpec((None, 3 * N_SUB, d), lambda i: (grp(i), 0, 0)),
                            pl.BlockSpec((N_SUB, d), lambda i: (0, 0)),
                            pl.BlockSpec((N_SUB, d), lambda i: (0, 0))],
        out_specs=out_specs,
        out_shape=out_shape,
        compiler_params=_params("parallel"),
        name=f"post{s}",
    )(*hs, o, mod3, g_post, g_pre)


def _mm_kernel(n_w, n_out, combine, x_ref, *refs):
    w_refs, o_refs, wb_refs = refs[:n_w], refs[n_w:n_w + n_out], refs[n_w + n_out:]

    @pl.when(pl.program_id(1) == 0)
    def _():
        for w_ref, wb_ref in zip(w_refs, wb_refs):
            wb_ref[...] = w_ref[...].astype(BF16)

    x = x_ref[...]
    ys = [jnp.dot(x, wb_ref[...], preferred_element_type=F32) for wb_ref in wb_refs]
    for o_ref, y in zip(o_refs, combine(*ys)):
        o_ref[...] = y.astype(o_ref.dtype)


def _matmul(x, ws, combine, out_dtypes, n_cols, tn, name):
    m, k = x.shape
    w_specs = [pl.BlockSpec((k, tn), lambda j, i, c0=col0 // tn: (0, j + c0)) for _, col0 in ws]
    return pl.pallas_call(
        functools.partial(_mm_kernel, len(ws), len(out_dtypes), combine),
        grid=(n_cols // tn, m // MM_ROWS),
        in_specs=[pl.BlockSpec((MM_ROWS, k), lambda j, i: (i, 0))] + w_specs,
        out_specs=[pl.BlockSpec((MM_ROWS, tn), lambda j, i: (i, j))] * len(out_dtypes),
        out_shape=[jax.ShapeDtypeStruct((m, n_cols), dt) for dt in out_dtypes],
        scratch_shapes=[pltpu.VMEM((k, tn), BF16)] * len(ws),
        compiler_params=_params("arbitrary", "arbitrary"),
        name=name,
    )(x, *(w for w, _ in ws))


def _swiglu_combine(g, u):
    return ((g * jax.nn.sigmoid(g)) * u,)


def _identity_combine(y):
    return (y,)


def _conv_gate_combine(gb, gc, hc):
    return gb, gc * hc


def _ffn_up_loop_kernel(x_hbm, wg_ref, wu_ref, a_ref, xbuf, sem, wgb_ref, wub_ref):
    j = pl.program_id(0)
    n_j = pl.num_programs(0)
    tm = xbuf.shape[1]
    n_i = x_hbm.shape[0] // tm

    def x_copy(i, slot):
        return pltpu.make_async_copy(x_hbm.at[pl.ds(pl.multiple_of(i * tm, tm), tm), :],
                                     xbuf.at[slot], sem.at[slot])

    @pl.when(j == 0)
    def _():
        x_copy(0, 0).start()

    wgb_ref[...] = wg_ref[...].astype(BF16)
    wub_ref[...] = wu_ref[...].astype(BF16)

    def body(i, carry):
        slot = i % 2
        x_copy(i, slot).wait()

        @pl.when(i + 1 < n_i)
        def _():
            x_copy(i + 1, 1 - slot).start()

        @pl.when(jnp.logical_and(i + 1 == n_i, j + 1 < n_j))
        def _():
            x_copy(0, 1 - slot).start()

        x = xbuf[slot]
        g = jnp.dot(x, wgb_ref[...], preferred_element_type=F32)
        u = jnp.dot(x, wub_ref[...], preferred_element_type=F32)
        (a,) = _swiglu_combine(g, u)
        a_ref[pl.ds(pl.multiple_of(i * tm, tm), tm), :] = a.astype(a_ref.dtype)
        return carry

    lax.fori_loop(0, n_i, body, 0)


def _ffn_up_loop(x, wg, wu):
    m, k = x.shape
    f = wg.shape[1]
    tn = MXU_COLS
    assert (m // MM_ROWS) % 2 == 0
    return pl.pallas_call(
        _ffn_up_loop_kernel,
        grid=(f // tn,),
        in_specs=[pl.BlockSpec(memory_space=pl.ANY),
                  pl.BlockSpec((k, tn), lambda j: (0, j)),
                  pl.BlockSpec((k, tn), lambda j: (0, j))],
        out_specs=pl.BlockSpec((m, tn), lambda j: (0, j)),
        out_shape=jax.ShapeDtypeStruct((m, f), BF16),
        scratch_shapes=[pltpu.VMEM((2, MM_ROWS, k), BF16), pltpu.SemaphoreType.DMA((2,)),
                        pltpu.VMEM((k, tn), BF16), pltpu.VMEM((k, tn), BF16)],
        compiler_params=_params("arbitrary"),
        name="ffn_up_loop",
    )(x, wg, wu)


def _ffn_up_mod_kernel(n_side, x_ref, wg_ref, wu_ref, c_ref, wm_ref, b_ref, a_ref, m_ref, wgb_ref, wub_ref):
    _mm_kernel(2, 1, _swiglu_combine, x_ref, wg_ref, wu_ref, a_ref, wgb_ref, wub_ref)

    @pl.when(pl.program_id(0) * pl.num_programs(1) + pl.program_id(1) < n_side)
    def _():
        _mod_tile(c_ref, wm_ref, b_ref, m_ref)


def _ffn_up_with_modulation(x, wg, wu, cond, w_mod, b_mod2, col0):
    m, k = x.shape
    f = wg.shape[1]
    tn = MXU_COLS
    tmod = 128
    n_i = m // MM_ROWS
    n_side = (w_mod.shape[1] - col0) // tmod
    assert n_side <= (f // tn) * n_i
    side = lambda j, i: (0, col0 // tmod + jnp.minimum(j * n_i + i, n_side - 1))
    return pl.pallas_call(
        functools.partial(_ffn_up_mod_kernel, n_side),
        grid=(f // tn, n_i),
        in_specs=[pl.BlockSpec((MM_ROWS, k), lambda j, i: (i, 0)),
                  pl.BlockSpec((k, tn), lambda j, i: (0, j)),
                  pl.BlockSpec((k, tn), lambda j, i: (0, j)),
                  pl.BlockSpec(cond.shape, lambda j, i: (0, 0)),
                  pl.BlockSpec((w_mod.shape[0], tmod), side),
                  pl.BlockSpec((1, tmod), side)],
        out_specs=[pl.BlockSpec((MM_ROWS, tn), lambda j, i: (i, j)),
                   pl.BlockSpec((cond.shape[0], tmod), lambda j, i: (0, jnp.minimum(j * n_i + i, n_side - 1)))],
        out_shape=[jax.ShapeDtypeStruct((m, f), BF16),
                   jax.ShapeDtypeStruct((cond.shape[0], n_side * tmod), F32)],
        scratch_shapes=[pltpu.VMEM((k, tn), BF16)] * 2,
        compiler_params=_params("arbitrary", "arbitrary"),
        name="ffn_up_mod",
    )(x, wg, wu, cond, w_mod, b_mod2)


def _down_kernel(a_ref, w_ref, o_ref):
    o_ref[...] = jnp.dot(a_ref[...], w_ref[...].astype(BF16), preferred_element_type=F32)


def _down(a, w):
    m, f = a.shape
    d = w.shape[1]
    tn = MXU_COLS
    return pl.pallas_call(
        _down_kernel,
        grid=(m // MM_ROWS, d // tn),
        in_specs=[pl.BlockSpec((MM_ROWS, f), lambda i, j: (i, 0), pipeline_mode=pl.Buffered(1)),
                  pl.BlockSpec((f, tn), lambda i, j: (0, j))],
        out_specs=pl.BlockSpec((MM_ROWS, tn), lambda i, j: (i, j)),
        out_shape=jax.ShapeDtypeStruct((m, d), F32),
        compiler_params=_params("parallel", "arbitrary"),
        name="ffn_down",
    )(a, w)


def _softmax_with_sink(s, sink):
    m = jnp.maximum(jnp.max(s, axis=-1, keepdims=True), sink)
    p = jnp.exp(s - m)
    denom = jnp.sum(p, axis=-1, keepdims=True) + jnp.exp(sink - m)
    return p * (1.0 / denom)


def _qk(q, k):
    return lax.dot_general(q, k, (((1,), (1,)), ((), ())), preferred_element_type=F32) * ATTN_SCALE


def _group_sinks(sink_ref, h, rows_per_head):
    head = lax.broadcasted_iota(jnp.int32, (GQA_GROUP * rows_per_head, 1), 0) // rows_per_head
    col = jnp.full(head.shape, sink_ref[h * GQA_GROUP], F32)
    for g in range(1, GQA_GROUP):
        col = jnp.where(head == g, sink_ref[h * GQA_GROUP + g], col)
    return col


def _stack_heads(q_ref, prep):
    return jnp.concatenate([prep(q_ref[:, g * HEAD_DIM:(g + 1) * HEAD_DIM]).astype(BF16)
                            for g in range(GQA_GROUP)], axis=0)


def _unstack_heads(o, o_ref):
    rows = o_ref.shape[0]
    for g in range(GQA_GROUP):
        o_ref[:, g * HEAD_DIM:(g + 1) * HEAD_DIM] = o[g * rows:(g + 1) * rows, :]


def _ctx_attn_kernel(sink_ref, q_ref, k_ref, v_ref, o_ref, ks_ref, vs_ref):
    ks_ref[...] = k_ref[...]
    vs_ref[...] = v_ref[...]
    qw = GQA_GROUP * HEAD_DIM
    for h in range(N_KV_HEADS):
        kv_cols = slice(h * HEAD_DIM, (h + 1) * HEAD_DIM)
        q = _stack_heads(q_ref.at[:, h * qw:(h + 1) * qw], lambda x: x)
        s = _qk(q, k_ref[:, kv_cols].astype(BF16))
        p = _softmax_with_sink(s, _group_sinks(sink_ref, h, q_ref.shape[0]))
        o = jnp.dot(p.astype(BF16), v_ref[:, kv_cols].astype(BF16), preferred_element_type=F32)
        _unstack_heads(o, o_ref.at[:, h * qw:(h + 1) * qw])


def _ctx_attention(z, sink, n_batch, seq):
    q_w = N_Q_HEADS * HEAD_DIM
    kv_w = N_KV_HEADS * HEAD_DIM
    state = jax.ShapeDtypeStruct((n_batch, seq, kv_w), F32)
    return pl.pallas_call(
        _ctx_attn_kernel,
        grid=(n_batch,),
        in_specs=[pl.BlockSpec(memory_space=pltpu.SMEM),
                  pl.BlockSpec((seq, q_w), lambda b: (b, 0)),
                  pl.BlockSpec((seq, kv_w), lambda b: (b, q_w // kv_w)),
                  pl.BlockSpec((seq, kv_w), lambda b: (b, q_w // kv_w + 1))],
        out_specs=[pl.BlockSpec((seq, q_w), lambda b: (b, 0)),
                   pl.BlockSpec((None, seq, kv_w), lambda b: (b, 0, 0)),
                   pl.BlockSpec((None, seq, kv_w), lambda b: (b, 0, 0))],
        out_shape=[jax.ShapeDtypeStruct((n_batch * seq, q_w), F32), state, state],
        compiler_params=_params("parallel"),
        name="ctx_attention",
    )(sink, z, z, z)


def _rope(x, cos, sin_signed):
    lane = lax.broadcasted_iota(jnp.int32, x.shape, 1)
    first = (lane % ROPE_AXIS_DIM) < ROPE_PAIRS
    partner = jnp.where(first, pltpu.roll(x, HEAD_DIM - ROPE_PAIRS, axis=1),
                        pltpu.roll(x, ROPE_PAIRS, axis=1))
    return x * cos + partner * sin_signed


def _lat_attn_kernel(sink_ref, q_ref, k_ref, v_ref, kc_ref, vc_ref, cos_ref, sin_ref, o_ref,
                     kr_ref, vb_ref, kcb_ref, vcb_ref):
    h = pl.program_id(1)
    i = pl.program_id(2)
    n_blk = pl.num_programs(2)
    blk = q_ref.shape[0]

    @pl.when(i == 0)
    def _():
        kr_ref[...] = _rope(k_ref[...], cos_ref[...], sin_ref[...]).astype(BF16)
        vb_ref[...] = v_ref[...].astype(BF16)
        kcb_ref[...] = kc_ref[...].astype(BF16)
        vcb_ref[...] = vc_ref[...].astype(BF16)

    q0 = pl.multiple_of(i * blk, blk)
    w0 = pl.multiple_of(jnp.clip(i - 1, 0, n_blk - 3) * blk, blk)
    cos = cos_ref[pl.ds(q0, blk), :]
    sin = sin_ref[pl.ds(q0, blk), :]
    q = _stack_heads(q_ref, lambda x: _rope(x, cos, sin))
    rows = GQA_GROUP * blk
    q_pos = q0 + lax.broadcasted_iota(jnp.int32, (rows, 3 * blk), 0) % blk
    k_pos = w0 + lax.broadcasted_iota(jnp.int32, (rows, 3 * blk), 1)
    in_window = jnp.abs(k_pos - q_pos) <= WINDOW
    s = jnp.concatenate([jnp.where(in_window, _qk(q, kr_ref[pl.ds(w0, 3 * blk), :]), NEG),
                         _qk(q, kcb_ref[...])], axis=-1)
    p = _softmax_with_sink(s, _group_sinks(sink_ref, h, blk)).astype(BF16)
    o = (jnp.dot(p[:, :3 * blk], vb_ref[pl.ds(w0, 3 * blk), :], preferred_element_type=F32)
         + jnp.dot(p[:, 3 * blk:], vcb_ref[...], preferred_element_type=F32))
    _unstack_heads(o, o_ref)


def _lat_attention(z, cache_k, cache_v, sink, cos, sin_signed, row0, n_batch, seq):
    blk = WINDOW
    n_blk = seq // blk
    past = cache_k.shape[1]
    qw = GQA_GROUP * HEAD_DIM
    k_col = N_Q_HEADS
    v_col = k_col + N_KV_HEADS
    return pl.pallas_call(
        _lat_attn_kernel,
        grid=(n_batch, N_KV_HEADS, n_blk),
        in_specs=[pl.BlockSpec(memory_space=pltpu.SMEM),
                  pl.BlockSpec((blk, qw), lambda b, h, i: (row0 // blk + b * n_blk + i, h)),
                  pl.BlockSpec((seq, HEAD_DIM), lambda b, h, i: (row0 // seq + b, k_col + h)),
                  pl.BlockSpec((seq, HEAD_DIM), lambda b, h, i: (row0 // seq + b, v_col + h)),
                  pl.BlockSpec((None, past, HEAD_DIM), lambda b, h, i: (b, 0, h)),
                  pl.BlockSpec((None, past, HEAD_DIM), lambda b, h, i: (b, 0, h)),
                  pl.BlockSpec((seq, HEAD_DIM), lambda b, h, i: (0, 0)),
                  pl.BlockSpec((seq, HEAD_DIM), lambda b, h, i: (0, 0))],
        out_specs=pl.BlockSpec((blk, qw), lambda b, h, i: (b * n_blk + i, h)),
        out_shape=jax.ShapeDtypeStruct((n_batch * seq, N_Q_HEADS * HEAD_DIM), F32),
        scratch_shapes=[pltpu.VMEM((seq, HEAD_DIM), BF16), pltpu.VMEM((seq, HEAD_DIM), BF16),
                        pltpu.VMEM((past, HEAD_DIM), BF16), pltpu.VMEM((past, HEAD_DIM), BF16)],
        compiler_params=_params("parallel", "parallel", "arbitrary"),
        name="latent_attention",
    )(sink, z, z, z, cache_k, cache_v, cos, sin_signed)


def _rope_tables(seq):
    t = np.arange(seq)
    pos = np.stack([t // GRID_W, t % GRID_W], axis=1).astype(np.float64)
    inv_freq = ROPE_THETA ** (-np.arange(ROPE_PAIRS, dtype=np.float64) / ROPE_PAIRS)
    ang = pos[:, :, None] * inv_freq[None, None, :]
    cos, sin = np.cos(ang), np.sin(ang)
    cos = np.concatenate([cos, cos], axis=-1).reshape(seq, HEAD_DIM)
    sin = np.concatenate([-sin, sin], axis=-1).reshape(seq, HEAD_DIM)
    return jnp.asarray(cos, F32), jnp.asarray(sin, F32)


def _conv_merge_kernel(n_ctx, lat_seq, ac_ref, al_ref, gb_ref, u_ref, up_ref, un_ref, w_ref,
                       ga_ref, gc_ref, o_ref):
    t = pl.program_id(0)
    rows, wa = ac_ref.shape
    r0 = t * rows
    is_ctx = r0 < n_ctx
    is_first = jnp.logical_or(is_ctx, (r0 - n_ctx) % lat_seq == 0)
    is_last = jnp.logical_or(is_ctx, (r0 + rows - n_ctx) % lat_seq == 0)
    u = u_ref[...]
    u_before = jnp.where(is_first, 0.0, up_ref[pl.ds(up_ref.shape[0] - 1, 1), :])
    u_after = jnp.where(is_last, 0.0, un_ref[pl.ds(0, 1), :])
    row = lax.broadcasted_iota(jnp.int32, u.shape, 0)
    u_prev = jnp.where(row == 0, u_before, pltpu.roll(u, 1, axis=0))
    u_next = jnp.where(row == rows - 1, u_after, pltpu.roll(u, rows - 1, axis=0))
    y = u_prev * w_ref[pl.ds(0, 1), :] + u * w_ref[pl.ds(1, 1), :] + u_next * w_ref[pl.ds(2, 1), :]
    attn = jnp.where(is_ctx, ac_ref[...], al_ref[...])
    o_ref[:, :wa] = _rms(attn, ga_ref[...]).astype(BF16)
    o_ref[:, wa:] = _rms(gb_ref[...] * y, gc_ref[...]).astype(BF16)


def _conv_merge(attn_ctx, attn_lat, gb, u, w_conv, g_a, g_c, ctx_seq, lat_seq):
    assert ctx_seq == ROW_TILE
    n, wc = u.shape
    wa = attn_ctx.shape[1]
    n_ctx = attn_ctx.shape[0]
    t_ctx = n_ctx // ROW_TILE
    t_lat = attn_lat.shape[0] // ROW_TILE
    halo = 8
    per = ROW_TILE // halo
    last_halo = n // halo - 1
    return pl.pallas_call(
        functools.partial(_conv_merge_kernel, n_ctx, lat_seq),
        grid=(n // ROW_TILE,),
        in_specs=[pl.BlockSpec((ROW_TILE, wa), lambda t: (jnp.minimum(t, t_ctx - 1), 0)),
                  pl.BlockSpec((ROW_TILE, wa), lambda t: (jnp.clip(t - t_ctx, 0, t_lat - 1), 0)),
                  pl.BlockSpec((ROW_TILE, wc), lambda t: (t, 0)),
                  pl.BlockSpec((ROW_TILE, wc), lambda t: (t, 0)),
                  pl.BlockSpec((halo, wc), lambda t: (jnp.maximum(t * per - 1, 0), 0)),
                  pl.BlockSpec((halo, wc), lambda t: (jnp.minimum((t + 1) * per, last_halo), 0)),
                  pl.BlockSpec((w_conv.shape[0], wc), lambda t: (0, 0)),
                  pl.BlockSpec((1, wa), lambda t: (0, 0)),
                  pl.BlockSpec((1, wc), lambda t: (0, 0))],
        out_specs=pl.BlockSpec((ROW_TILE, wa + wc), lambda t: (t, 0)),
        out_shape=jax.ShapeDtypeStruct((n, wa + wc), BF16),
        compiler_params=_params("parallel"),
        name="conv_merge",
    )(attn_ctx, attn_lat, gb, u, u, u, w_conv, g_a.reshape(1, wa), g_c.reshape(1, wc))


def kernel(x_prompt, x_sample, c, cache_k, cache_v, c_ctx, w_mod, b_mod, g_pre, g_post, w_in, w_conv,
           sink, g_attn_out, g_conv_out, w_o, w_ffn1_gate, w_ffn1_up, w_ffn1_down,
           w_ffn2_gate, w_ffn2_up, w_ffn2_down):
    batch, seq, d = x_prompt.shape
    dec_batch, dec_seq, _ = x_sample.shape
    depth = w_mod.shape[0]
    past = cache_k.shape[2]
    n_ctx = batch * seq
    n_lat = dec_batch * dec_seq
    attn_w = N_Q_HEADS * HEAD_DIM
    kv_w = N_KV_HEADS * HEAD_DIM
    conv_w = d - attn_w
    d_ff = w_ffn1_gate.shape[2]

    hs = (x_prompt.reshape(n_ctx, d), x_sample.reshape(n_lat, d))
    cond = jnp.concatenate([c_ctx[None, :], c, jnp.zeros((8 - 1 - dec_batch, d), F32)], axis=0)
    cos, sin_signed = _rope_tables(dec_seq)
    post = functools.partial(_post, n_ctx=n_ctx, lat_seq=dec_seq)
    wide = 2 * MXU_COLS

    ks_new, vs_new = [], []
    for l in range(depth):
        b_mod2 = b_mod[l].reshape(1, -1)
        mod_head = _modulation(cond, w_mod[l], b_mod2, 2 * d)
        u = _pre(0, hs, mod_head.reshape(8, 2, d), g_pre[l], n_ctx, dec_seq)
        a, mod_tail = _ffn_up_with_modulation(u, w_ffn1_gate[l], w_ffn1_up[l], cond, w_mod[l], b_mod2, 2 * d)
        mod3 = jnp.concatenate([mod_head, mod_tail], axis=1).reshape(8, 3 * N_SUB, d)
        o = _down(a, w_ffn1_down[l])
        h, u = post(0, 0.5, hs, o, mod3, g_post[l], g_pre[l], with_pre=True)

        qkv_w = attn_w + 2 * kv_w
        (zqkv,) = _matmul(u, [(w_in[l], 0)], _identity_combine, [F32], qkv_w, wide, "in_proj_qkv")
        gb, uc = _matmul(u, [(w_in[l], qkv_w), (w_in[l], qkv_w + conv_w), (w_in[l], qkv_w + 2 * conv_w)],
                         _conv_gate_combine, [F32, F32], conv_w, MXU_COLS, "in_proj_conv")
        attn_ctx, k_new, v_new = _ctx_attention(zqkv, sink[l], batch, seq)
        attn_lat = _lat_attention(zqkv, cache_k[:, l].reshape(dec_batch, past, kv_w),
                                  cache_v[:, l].reshape(dec_batch, past, kv_w),
                                  sink[l], cos, sin_signed, n_ctx, dec_batch, dec_seq)
        mrg = _conv_merge(attn_ctx, attn_lat, gb, uc, w_conv[l], g_attn_out[l], g_conv_out[l], seq, dec_seq)
        (o,) = _matmul(mrg, [(w_o[l], 0)], _identity_combine, [F32], d, wide, "out_proj")
        h, u = post(1, 1.0, (h,), o, mod3, g_post[l], g_pre[l], with_pre=True)

        a = _ffn_up_loop(u, w_ffn2_gate[l], w_ffn2_up[l])
        o = _down(a, w_ffn2_down[l])
        if l + 1 < depth:
            (h,) = post(2, 0.5, (h,), o, mod3, g_post[l], g_pre[l], with_pre=False)
            hs = (h,)
        ks_new.append(k_new.reshape(batch, seq, N_KV_HEADS, HEAD_DIM))
        vs_new.append(v_new.reshape(batch, seq, N_KV_HEADS, HEAD_DIM))

    (y_prompt,) = post(2, 0.5, (h,), o, mod3, g_post[depth - 1], g_pre[depth - 1], with_pre=False,
                       row0=0, rows=n_ctx)
    (y_sample,) = post(2, 0.5, (h,), o, mod3, g_post[depth - 1], g_pre[depth - 1], with_pre=False,
                       row0=n_ctx, rows=n_lat)
    return (y_prompt.reshape(batch, seq, d), y_sample.reshape(dec_batch, dec_seq, d),
            jnp.stack(ks_new, axis=1), jnp.stack(vs_new, axis=1))
```

```python
import functools

import jax
import jax.numpy as jnp
import numpy as np
from jax import lax
from jax.experimental import pallas as pl
from jax.experimental.pallas import tpu as pltpu

F32 = jnp.float32
BF16 = jnp.bfloat16

N_Q_HEADS = 16
N_KV_HEADS = 4
GQA_GROUP = N_Q_HEADS // N_KV_HEADS
HEAD_DIM = 128
GRID_W = 64
WINDOW = 128
ROPE_THETA = 10000.0
ROPE_AXIS_DIM = HEAD_DIM // 2
ROPE_PAIRS = ROPE_AXIS_DIM // 2
N_SUB = 3
EPS = 1e-6
NEG = -1e30
ATTN_SCALE = HEAD_DIM ** -0.5

VMEM_LIMIT_BYTES = 60 * 1024 * 1024
MXU_COLS = 256
ROW_TILE = 256
MM_ROWS = 1024


def _params(*sem):
    return pltpu.CompilerParams(dimension_semantics=sem, vmem_limit_bytes=VMEM_LIMIT_BYTES)


def _rms(x, g):
    return x * lax.rsqrt(jnp.mean(x * x, axis=-1, keepdims=True) + EPS) * g


def _mod_tile(c_ref, w_ref, b_ref, o_ref):
    x = c_ref[...]
    x = (x * jax.nn.sigmoid(x)).astype(BF16)
    o_ref[...] = jnp.dot(x, w_ref[...].astype(BF16), preferred_element_type=F32) + b_ref[...]


def _modulation(cond, w_mod, b_mod2, n_cols):
    rows, d = cond.shape
    tn = 512
    return pl.pallas_call(
        _mod_tile,
        grid=(n_cols // tn,),
        in_specs=[pl.BlockSpec((rows, d), lambda j: (0, 0)),
                  pl.BlockSpec((d, tn), lambda j: (0, j)),
                  pl.BlockSpec((1, tn), lambda j: (0, j))],
        out_specs=pl.BlockSpec((rows, tn), lambda j: (0, j)),
        out_shape=jax.ShapeDtypeStruct((rows, n_cols), F32),
        compiler_params=_params("arbitrary"),
        name="modulation",
    )(cond, w_mod, b_mod2)


def _load_rows(h_refs, n_ctx_tiles):
    if len(h_refs) == 1:
        return h_refs[0][...]
    return jnp.where(pl.program_id(0) < n_ctx_tiles, h_refs[0][...], h_refs[1][...])


def _pre_kernel(s, n_h, n_ctx_tiles, *refs):
    h_refs, (m_ref, gpre_ref, u_ref) = refs[:n_h], refs[n_h:]
    shift = m_ref[pl.ds(3 * s, 1), :]
    scale = m_ref[pl.ds(3 * s + 1, 1), :]
    u = _rms(_load_rows(h_refs, n_ctx_tiles), gpre_ref[pl.ds(s, 1), :]) * (1.0 + scale) + shift
    u_ref[...] = u.astype(u_ref.dtype)


def _post_kernel(s, res_w, with_pre, n_h, n_ctx_tiles, *refs):
    h_refs, (o_ref, m_ref, gpost_ref, gpre_ref, hn_ref, *u_ref) = refs[:n_h], refs[n_h:]
    gate = m_ref[pl.ds(3 * s + 2, 1), :]
    hn = _load_rows(h_refs, n_ctx_tiles) + (res_w * gate) * _rms(o_ref[...], gpost_ref[pl.ds(s, 1), :])
    hn_ref[...] = hn
    if with_pre:
        shift = m_ref[pl.ds(3 * (s + 1), 1), :]
        scale = m_ref[pl.ds(3 * (s + 1) + 1, 1), :]
        u = _rms(hn, gpre_ref[pl.ds(s + 1, 1), :]) * (1.0 + scale) + shift
        u_ref[0][...] = u.astype(BF16)


def _group_of_tile(i, tile_rows, n_ctx, lat_seq):
    r0 = i * tile_rows
    return jnp.where(r0 < n_ctx, 0, 1 + (r0 - n_ctx) // lat_seq)


def _row_specs(hs, d, t0=0):
    if len(hs) == 1:
        return [pl.BlockSpec((ROW_TILE, d), lambda i: (i + t0, 0))], 0
    t_ctx, t_lat = (x.shape[0] // ROW_TILE for x in hs)
    return [pl.BlockSpec((ROW_TILE, d), lambda i: (jnp.minimum(i, t_ctx - 1), 0)),
            pl.BlockSpec((ROW_TILE, d), lambda i: (jnp.clip(i - t_ctx, 0, t_lat - 1), 0))], t_ctx


def _pre(s, hs, mod3, g_pre, n_ctx, lat_seq):
    n = sum(x.shape[0] for x in hs)
    d = hs[0].shape[1]
    grp = functools.partial(_group_of_tile, tile_rows=ROW_TILE, n_ctx=n_ctx, lat_seq=lat_seq)
    h_specs, t_ctx = _row_specs(hs, d)
    return pl.pallas_call(
        functools.partial(_pre_kernel, s, len(hs), t_ctx),
        grid=(n // ROW_TILE,),
        in_specs=h_specs + [pl.BlockSpec((None, mod3.shape[1], d), lambda i: (grp(i), 0, 0)),
                            pl.BlockSpec((N_SUB, d), lambda i: (0, 0))],
        out_specs=pl.BlockSpec((ROW_TILE, d), lambda i: (i, 0)),
        out_shape=jax.ShapeDtypeStruct((n, d), BF16),
        compiler_params=_params("parallel"),
        name=f"pre{s}",
    )(*hs, mod3, g_pre)


def _post(s, res_w, hs, o, mod3, g_post, g_pre, n_ctx, lat_seq, with_pre, row0=0, rows=None):
    d = o.shape[1]
    rows = o.shape[0] if rows is None else rows
    t0 = row0 // ROW_TILE
    grp = lambda i: _group_of_tile(i + t0, ROW_TILE, n_ctx, lat_seq)
    h_specs, t_ctx = _row_specs(hs, d, t0)
    out_shape = [jax.ShapeDtypeStruct((rows, d), F32)]
    out_specs = [pl.BlockSpec((ROW_TILE, d), lambda i: (i, 0))]
    if with_pre:
        out_shape.append(jax.ShapeDtypeStruct((rows, d), BF16))
        out_specs.append(pl.BlockSpec((ROW_TILE, d), lambda i: (i, 0)))
    return pl.pallas_call(
        functools.partial(_post_kernel, s, res_w, with_pre, len(hs), t_ctx),
        grid=(rows // ROW_TILE,),
        in_specs=h_specs + [pl.BlockSpec((ROW_TILE, d), lambda i: (i + t0, 0)),
                            pl.BlockSpec((None, 3 * N_SUB, d), lambda i: (grp(i), 0, 0)),
                            pl.BlockSpec((N_SUB, d), lambda i: (0, 0)),
                            pl.BlockSpec((N_SUB, d), lambda i: (0, 0))],
        out_specs=out_specs,
        out_shape=out_shape,
        compiler_params=_params("parallel"),
        name=f"post{s}",
    )(*hs, o, mod3, g_post, g_pre)


def _mm_kernel(slabs, n_w, n_out, combine, *refs):
    n_x = sum(n for n, _ in slabs)
    x_refs, refs = refs[:n_x], refs[n_x:]
    w_refs, o_refs, wb_refs = refs[:n_w], refs[n_w:n_w + n_out], refs[n_w + n_out:]

    @pl.when(pl.program_id(1) == 0)
    def _():
        for w_ref, wb_ref in zip(w_refs, wb_refs):
            wb_ref[...] = w_ref[...].astype(BF16)

    xs, r = [], 0
    for n, t_first in slabs:
        if n == 1:
            xs.append(x_refs[r][...])
        else:
            xs.append(jnp.where(pl.program_id(1) < t_first, x_refs[r][...], x_refs[r + 1][...]))
        r += n
    ys = []
    for wb_ref in wb_refs:
        k0, y = 0, None
        for x in xs:
            k1 = k0 + x.shape[1]
            part = jnp.dot(x, wb_ref[k0:k1, :], preferred_element_type=F32)
            y = part if y is None else y + part
            k0 = k1
        ys.append(y)
    for o_ref, y in zip(o_refs, combine(*ys)):
        o_ref[...] = y.astype(o_ref.dtype)


def _matmul(xs, ws, combine, out_dtypes, n_cols, tn, name):
    x_specs, x_args, slabs = [], [], []
    for x in xs:
        if isinstance(x, tuple):
            t_a, t_b = (p.shape[0] // MM_ROWS for p in x)
            x_specs += [pl.BlockSpec((MM_ROWS, x[0].shape[1]), lambda j, i, t=t_a: (jnp.minimum(i, t - 1), 0)),
                        pl.BlockSpec((MM_ROWS, x[1].shape[1]),
                                     lambda j, i, t=t_a, u=t_b: (jnp.clip(i - t, 0, u - 1), 0))]
            x_args += list(x)
            slabs.append((2, t_a))
        else:
            x_specs.append(pl.BlockSpec((MM_ROWS, x.shape[1]), lambda j, i: (i, 0)))
            x_args.append(x)
            slabs.append((1, 0))
    m = sum(p.shape[0] for p in xs[0]) if isinstance(xs[0], tuple) else xs[0].shape[0]
    k = sum((x[0] if isinstance(x, tuple) else x).shape[1] for x in xs)
    w_specs = [pl.BlockSpec((k, tn), lambda j, i, c0=col0 // tn: (0, j + c0)) for _, col0 in ws]
    return pl.pallas_call(
        functools.partial(_mm_kernel, tuple(slabs), len(ws), len(out_dtypes), combine),
        grid=(n_cols // tn, m // MM_ROWS),
        in_specs=x_specs + w_specs,
        out_specs=[pl.BlockSpec((MM_ROWS, tn), lambda j, i: (i, j))] * len(out_dtypes),
        out_shape=[jax.ShapeDtypeStruct((m, n_cols), dt) for dt in out_dtypes],
        scratch_shapes=[pltpu.VMEM((k, tn), BF16)] * len(ws),
        compiler_params=_params("arbitrary", "arbitrary"),
        name=name,
    )(*x_args, *(w for w, _ in ws))


def _swiglu_combine(g, u):
    return ((g * jax.nn.sigmoid(g)) * u,)


def _identity_combine(y):
    return (y,)


def _conv_gate_combine(gb, gc, hc):
    return gb, gc * hc


def _ffn_up_mod_kernel(n_side, x_ref, wg_ref, wu_ref, c_ref, wm_ref, b_ref, a_ref, m_ref, wgb_ref, wub_ref):
    _mm_kernel(((1, 0),), 2, 1, _swiglu_combine, x_ref, wg_ref, wu_ref, a_ref, wgb_ref, wub_ref)

    @pl.when(pl.program_id(0) * pl.num_programs(1) + pl.program_id(1) < n_side)
    def _():
        _mod_tile(c_ref, wm_ref, b_ref, m_ref)


def _ffn_up_with_modulation(x, wg, wu, cond, w_mod, b_mod2, col0):
    m, k = x.shape
    f = wg.shape[1]
    tn = MXU_COLS
    tmod = 128
    n_i = m // MM_ROWS
    n_side = (w_mod.shape[1] - col0) // tmod
    assert n_side <= (f // tn) * n_i
    side = lambda j, i: (0, col0 // tmod + jnp.minimum(j * n_i + i, n_side - 1))
    return pl.pallas_call(
        functools.partial(_ffn_up_mod_kernel, n_side),
        grid=(f // tn, n_i),
        in_specs=[pl.BlockSpec((MM_ROWS, k), lambda j, i: (i, 0)),
                  pl.BlockSpec((k, tn), lambda j, i: (0, j)),
                  pl.BlockSpec((k, tn), lambda j, i: (0, j)),
                  pl.BlockSpec(cond.shape, lambda j, i: (0, 0)),
                  pl.BlockSpec((w_mod.shape[0], tmod), side),
                  pl.BlockSpec((1, tmod), side)],
        out_specs=[pl.BlockSpec((MM_ROWS, tn), lambda j, i: (i, j)),
                   pl.BlockSpec((cond.shape[0], tmod), lambda j, i: (0, jnp.minimum(j * n_i + i, n_side - 1)))],
        out_shape=[jax.ShapeDtypeStruct((m, f), BF16),
                   jax.ShapeDtypeStruct((cond.shape[0], n_side * tmod), F32)],
        scratch_shapes=[pltpu.VMEM((k, tn), BF16)] * 2,
        compiler_params=_params("arbitrary", "arbitrary"),
        name="ffn_up_mod",
    )(x, wg, wu, cond, w_mod, b_mod2)


def _down_kernel(a_ref, w_ref, o_ref):
    o_ref[...] = jnp.dot(a_ref[...], w_ref[...].astype(BF16), preferred_element_type=F32)


def _down(a, w):
    m, f = a.shape
    d = w.shape[1]
    tn = MXU_COLS
    return pl.pallas_call(
        _down_kernel,
        grid=(m // MM_ROWS, d // tn),
        in_specs=[pl.BlockSpec((MM_ROWS, f), lambda i, j: (i, 0), pipeline_mode=pl.Buffered(1)),
                  pl.BlockSpec((f, tn), lambda i, j: (0, j))],
        out_specs=pl.BlockSpec((MM_ROWS, tn), lambda i, j: (i, j)),
        out_shape=jax.ShapeDtypeStruct((m, d), F32),
        compiler_params=_params("parallel", "arbitrary"),
        name="ffn_down",
    )(a, w)


def _softmax_with_sink(s, sink):
    m = jnp.maximum(jnp.max(s, axis=-1, keepdims=True), sink)
    p = jnp.exp(s - m)
    denom = jnp.sum(p, axis=-1, keepdims=True) + jnp.exp(sink - m)
    return p * (1.0 / denom)


def _qk(q, k):
    return lax.dot_general(q, k, (((1,), (1,)), ((), ())), preferred_element_type=F32) * ATTN_SCALE


def _group_sinks(sink_ref, h, rows_per_head):
    head = lax.broadcasted_iota(jnp.int32, (GQA_GROUP * rows_per_head, 1), 0) // rows_per_head
    col = jnp.full(head.shape, sink_ref[h * GQA_GROUP], F32)
    for g in range(1, GQA_GROUP):
        col = jnp.where(head == g, sink_ref[h * GQA_GROUP + g], col)
    return col


def _stack_heads(q_ref, prep):
    return jnp.concatenate([prep(q_ref[:, g * HEAD_DIM:(g + 1) * HEAD_DIM]).astype(BF16)
                            for g in range(GQA_GROUP)], axis=0)


def _unstack_heads(o, o_ref):
    rows = o_ref.shape[0]
    for g in range(GQA_GROUP):
        o_ref[:, g * HEAD_DIM:(g + 1) * HEAD_DIM] = o[g * rows:(g + 1) * rows, :]


def _ctx_attn_kernel(sink_ref, q_ref, k_ref, v_ref, g_ref, o_ref, ks_ref, vs_ref, acc_ref):
    ks_ref[...] = k_ref[...]
    vs_ref[...] = v_ref[...]
    qw = GQA_GROUP * HEAD_DIM
    for h in range(N_KV_HEADS):
        kv_cols = slice(h * HEAD_DIM, (h + 1) * HEAD_DIM)
        q = _stack_heads(q_ref.at[:, h * qw:(h + 1) * qw], lambda x: x)
        s = _qk(q, k_ref[:, kv_cols].astype(BF16))
        p = _softmax_with_sink(s, _group_sinks(sink_ref, h, q_ref.shape[0]))
        o = jnp.dot(p.astype(BF16), v_ref[:, kv_cols].astype(BF16), preferred_element_type=F32)
        _unstack_heads(o, acc_ref.at[:, h * qw:(h + 1) * qw])
    o_ref[...] = _rms(acc_ref[...], g_ref[...]).astype(o_ref.dtype)


def _ctx_attention(z, sink, g_a, n_batch, seq):
    q_w = N_Q_HEADS * HEAD_DIM
    kv_w = N_KV_HEADS * HEAD_DIM
    state = jax.ShapeDtypeStruct((n_batch, seq, kv_w), F32)
    return pl.pallas_call(
        _ctx_attn_kernel,
        grid=(n_batch,),
        in_specs=[pl.BlockSpec(memory_space=pltpu.SMEM),
                  pl.BlockSpec((seq, q_w), lambda b: (b, 0)),
                  pl.BlockSpec((seq, kv_w), lambda b: (b, q_w // kv_w)),
                  pl.BlockSpec((seq, kv_w), lambda b: (b, q_w // kv_w + 1)),
                  pl.BlockSpec((1, q_w), lambda b: (0, 0))],
        out_specs=[pl.BlockSpec((seq, q_w), lambda b: (b, 0)),
                   pl.BlockSpec((None, seq, kv_w), lambda b: (b, 0, 0)),
                   pl.BlockSpec((None, seq, kv_w), lambda b: (b, 0, 0))],
        out_shape=[jax.ShapeDtypeStruct((n_batch * seq, q_w), BF16), state, state],
        scratch_shapes=[pltpu.VMEM((seq, q_w), F32)],
        compiler_params=_params("parallel"),
        name="ctx_attention",
    )(sink, z, z, z, g_a.reshape(1, q_w))


def _rope(x, cos, sin_signed):
    lane = lax.broadcasted_iota(jnp.int32, x.shape, 1)
    first = (lane % ROPE_AXIS_DIM) < ROPE_PAIRS
    partner = jnp.where(first, pltpu.roll(x, HEAD_DIM - ROPE_PAIRS, axis=1),
                        pltpu.roll(x, ROPE_PAIRS, axis=1))
    return x * cos + partner * sin_signed


def _lat_attn_kernel(sink_ref, q_ref, k_ref, v_ref, kc_ref, vc_ref, cos_ref, sin_ref, g_ref, o_ref,
                     kr_ref, vb_ref, kcb_ref, vcb_ref, acc_ref):
    i = pl.program_id(1)
    n_blk = pl.num_programs(1)
    blk = q_ref.shape[0]
    qw = GQA_GROUP * HEAD_DIM

    @pl.when(i == 0)
    def _():
        for h in range(N_KV_HEADS):
            kv_cols = slice(h * HEAD_DIM, (h + 1) * HEAD_DIM)
            kr_ref[:, kv_cols] = _rope(k_ref[:, kv_cols], cos_ref[...], sin_ref[...]).astype(BF16)
        vb_ref[...] = v_ref[...].astype(BF16)
        kcb_ref[...] = kc_ref[...].astype(BF16)
        vcb_ref[...] = vc_ref[...].astype(BF16)

    q0 = pl.multiple_of(i * blk, blk)
    w0 = pl.multiple_of(jnp.clip(i - 1, 0, n_blk - 3) * blk, blk)
    cos = cos_ref[pl.ds(q0, blk), :]
    sin = sin_ref[pl.ds(q0, blk), :]
    rows = GQA_GROUP * blk
    q_pos = q0 + lax.broadcasted_iota(jnp.int32, (rows, 3 * blk), 0) % blk
    k_pos = w0 + lax.broadcasted_iota(jnp.int32, (rows, 3 * blk), 1)
    in_window = jnp.abs(k_pos - q_pos) <= WINDOW
    for h in range(N_KV_HEADS):
        kv_cols = slice(h * HEAD_DIM, (h + 1) * HEAD_DIM)
        q = _stack_heads(q_ref.at[:, h * qw:(h + 1) * qw], lambda x: _rope(x, cos, sin))
        s = jnp.concatenate([jnp.where(in_window, _qk(q, kr_ref[pl.ds(w0, 3 * blk), kv_cols]), NEG),
                             _qk(q, kcb_ref[:, kv_cols])], axis=-1)
        p = _softmax_with_sink(s, _group_sinks(sink_ref, h, blk)).astype(BF16)
        o = (jnp.dot(p[:, :3 * blk], vb_ref[pl.ds(w0, 3 * blk), kv_cols], preferred_element_type=F32)
             + jnp.dot(p[:, 3 * blk:], vcb_ref[:, kv_cols], preferred_element_type=F32))
        _unstack_heads(o, acc_ref.at[:, h * qw:(h + 1) * qw])
    o_ref[...] = _rms(acc_ref[...], g_ref[...]).astype(o_ref.dtype)


def _lat_attention(z, cache_k, cache_v, sink, g_a, cos, sin_signed, row0, n_batch, seq):
    blk = WINDOW
    n_blk = seq // blk
    past = cache_k.shape[1]
    q_w = N_Q_HEADS * HEAD_DIM
    kv_w = N_KV_HEADS * HEAD_DIM
    return pl.pallas_call(
        _lat_attn_kernel,
        grid=(n_batch, n_blk),
        in_specs=[pl.BlockSpec(memory_space=pltpu.SMEM),
                  pl.BlockSpec((blk, q_w), lambda b, i: (row0 // blk + b * n_blk + i, 0)),
                  pl.BlockSpec((seq, kv_w), lambda b, i: (row0 // seq + b, q_w // kv_w)),
                  pl.BlockSpec((seq, kv_w), lambda b, i: (row0 // seq + b, q_w // kv_w + 1)),
                  pl.BlockSpec((None, past, kv_w), lambda b, i: (b, 0, 0)),
                  pl.BlockSpec((None, past, kv_w), lambda b, i: (b, 0, 0)),
                  pl.BlockSpec((seq, HEAD_DIM), lambda b, i: (0, 0)),
                  pl.BlockSpec((seq, HEAD_DIM), lambda b, i: (0, 0)),
                  pl.BlockSpec((1, q_w), lambda b, i: (0, 0))],
        out_specs=pl.BlockSpec((blk, q_w), lambda b, i: (b * n_blk + i, 0)),
        out_shape=jax.ShapeDtypeStruct((n_batch * seq, q_w), BF16),
        scratch_shapes=[pltpu.VMEM((seq, kv_w), BF16), pltpu.VMEM((seq, kv_w), BF16),
                        pltpu.VMEM((past, kv_w), BF16), pltpu.VMEM((past, kv_w), BF16),
                        pltpu.VMEM((blk, q_w), F32)],
        compiler_params=_params("parallel", "arbitrary"),
        name="latent_attention",
    )(sink, z, z, z, cache_k, cache_v, cos, sin_signed, g_a.reshape(1, q_w))


def _rope_tables(seq):
    t = np.arange(seq)
    pos = np.stack([t // GRID_W, t % GRID_W], axis=1).astype(np.float64)
    inv_freq = ROPE_THETA ** (-np.arange(ROPE_PAIRS, dtype=np.float64) / ROPE_PAIRS)
    ang = pos[:, :, None] * inv_freq[None, None, :]
    cos, sin = np.cos(ang), np.sin(ang)
    cos = np.concatenate([cos, cos], axis=-1).reshape(seq, HEAD_DIM)
    sin = np.concatenate([-sin, sin], axis=-1).reshape(seq, HEAD_DIM)
    return jnp.asarray(cos, F32), jnp.asarray(sin, F32)


def _conv_norm_kernel(n_ctx, lat_seq, gb_ref, u_ref, up_ref, un_ref, w_ref, gc_ref, o_ref):
    t = pl.program_id(0)
    rows = u_ref.shape[0]
    r0 = t * rows
    is_ctx = r0 < n_ctx
    is_first = jnp.logical_or(is_ctx, (r0 - n_ctx) % lat_seq == 0)
    is_last = jnp.logical_or(is_ctx, (r0 + rows - n_ctx) % lat_seq == 0)
    u = u_ref[...]
    u_before = jnp.where(is_first, 0.0, up_ref[pl.ds(up_ref.shape[0] - 1, 1), :])
    u_after = jnp.where(is_last, 0.0, un_ref[pl.ds(0, 1), :])
    row = lax.broadcasted_iota(jnp.int32, u.shape, 0)
    u_prev = jnp.where(row == 0, u_before, pltpu.roll(u, 1, axis=0))
    u_next = jnp.where(row == rows - 1, u_after, pltpu.roll(u, rows - 1, axis=0))
    y = u_prev * w_ref[pl.ds(0, 1), :] + u * w_ref[pl.ds(1, 1), :] + u_next * w_ref[pl.ds(2, 1), :]
    o_ref[...] = _rms(gb_ref[...] * y, gc_ref[...]).astype(o_ref.dtype)


def _conv_norm(gb, u, w_conv, g_c, n_ctx, ctx_seq, lat_seq):
    assert ctx_seq == ROW_TILE
    n, wc = u.shape
    halo = 8
    per = ROW_TILE // halo
    last_halo = n // halo - 1
    return pl.pallas_call(
        functools.partial(_conv_norm_kernel, n_ctx, lat_seq),
        grid=(n // ROW_TILE,),
        in_specs=[pl.BlockSpec((ROW_TILE, wc), lambda t: (t, 0)),
                  pl.BlockSpec((ROW_TILE, wc), lambda t: (t, 0)),
                  pl.BlockSpec((halo, wc), lambda t: (jnp.maximum(t * per - 1, 0), 0)),
                  pl.BlockSpec((halo, wc), lambda t: (jnp.minimum((t + 1) * per, last_halo), 0)),
                  pl.BlockSpec((w_conv.shape[0], wc), lambda t: (0, 0)),
                  pl.BlockSpec((1, wc), lambda t: (0, 0))],
        out_specs=pl.BlockSpec((ROW_TILE, wc), lambda t: (t, 0)),
        out_shape=jax.ShapeDtypeStruct((n, wc), BF16),
        compiler_params=_params("parallel"),
        name="conv_norm",
    )(gb, u, u, u, w_conv, g_c.reshape(1, wc))


def kernel(x_prompt, x_sample, c, cache_k, cache_v, c_ctx, w_mod, b_mod, g_pre, g_post, w_in, w_conv,
           sink, g_attn_out, g_conv_out, w_o, w_ffn1_gate, w_ffn1_up, w_ffn1_down,
           w_ffn2_gate, w_ffn2_up, w_ffn2_down):
    batch, seq, d = x_prompt.shape
    dec_batch, dec_seq, _ = x_sample.shape
    depth = w_mod.shape[0]
    past = cache_k.shape[2]
    n_ctx = batch * seq
    n_lat = dec_batch * dec_seq
    attn_w = N_Q_HEADS * HEAD_DIM
    kv_w = N_KV_HEADS * HEAD_DIM
    conv_w = d - attn_w
    d_ff = w_ffn1_gate.shape[2]

    hs = (x_prompt.reshape(n_ctx, d), x_sample.reshape(n_lat, d))
    cond = jnp.concatenate([c_ctx[None, :], c, jnp.zeros((8 - 1 - dec_batch, d), F32)], axis=0)
    cos, sin_signed = _rope_tables(dec_seq)
    post = functools.partial(_post, n_ctx=n_ctx, lat_seq=dec_seq)
    wide = 2 * MXU_COLS

    ks_new, vs_new = [], []
    for l in range(depth):
        b_mod2 = b_mod[l].reshape(1, -1)
        mod_head = _modulation(cond, w_mod[l], b_mod2, 2 * d)
        u = _pre(0, hs, mod_head.reshape(8, 2, d), g_pre[l], n_ctx, dec_seq)
        a, mod_tail = _ffn_up_with_modulation(u, w_ffn1_gate[l], w_ffn1_up[l], cond, w_mod[l], b_mod2, 2 * d)
        mod3 = jnp.concatenate([mod_head, mod_tail], axis=1).reshape(8, 3 * N_SUB, d)
        o = _down(a, w_ffn1_down[l])
        h, u = post(0, 0.5, hs, o, mod3, g_post[l], g_pre[l], with_pre=True)

        qkv_w = attn_w + 2 * kv_w
        (zqkv,) = _matmul([u], [(w_in[l], 0)], _identity_combine, [F32], qkv_w, wide, "in_proj_qkv")
        gb, uc = _matmul([u], [(w_in[l], qkv_w), (w_in[l], qkv_w + conv_w), (w_in[l], qkv_w + 2 * conv_w)],
                         _conv_gate_combine, [F32, F32], conv_w, MXU_COLS, "in_proj_conv")
        attn_ctx, k_new, v_new = _ctx_attention(zqkv, sink[l], g_attn_out[l], batch, seq)
        attn_lat = _lat_attention(zqkv, cache_k[:, l].reshape(dec_batch, past, kv_w),
                                  cache_v[:, l].reshape(dec_batch, past, kv_w),
                                  sink[l], g_attn_out[l], cos, sin_signed, n_ctx, dec_batch, dec_seq)
        conv_n = _conv_norm(gb, uc, w_conv[l], g_conv_out[l], n_ctx, seq, dec_seq)
        (o,) = _matmul([(attn_ctx, attn_lat), conv_n], [(w_o[l], 0)], _identity_combine, [F32], d, wide,
                       "out_proj")
        h, u = post(1, 1.0, (h,), o, mod3, g_post[l], g_pre[l], with_pre=True)

        (a,) = _matmul([u], [(w_ffn2_gate[l], 0), (w_ffn2_up[l], 0)], _swiglu_combine, [BF16], d_ff,
                       MXU_COLS, "ffn_up")
        o = _down(a, w_ffn2_down[l])
        if l + 1 < depth:
            (h,) = post(2, 0.5, (h,), o, mod3, g_post[l], g_pre[l], with_pre=False)
            hs = (h,)
        ks_new.append(k_new.reshape(batch, seq, N_KV_HEADS, HEAD_DIM))
        vs_new.append(v_new.reshape(batch, seq, N_KV_HEADS, HEAD_DIM))

    (y_prompt,) = post(2, 0.5, (h,), o, mod3, g_post[depth - 1], g_pre[depth - 1], with_pre=False,
                       row0=0, rows=n_ctx)
    (y_sample,) = post(2, 0.5, (h,), o, mod3, g_post[depth - 1], g_pre[depth - 1], with_pre=False,
                       row0=n_ctx, rows=n_lat)
    return (y_prompt.reshape(batch, seq, d), y_sample.reshape(dec_batch, dec_seq, d),
            jnp.stack(ks_new, axis=1), jnp.stack(vs_new, axis=1))
```

```python
import functools

import jax
import jax.numpy as jnp
import numpy as np
from jax import lax
from jax.experimental import pallas as pl
from jax.experimental.pallas import tpu as pltpu

F32 = jnp.float32
BF16 = jnp.bfloat16

N_Q_HEADS = 16
N_KV_HEADS = 4
GQA_GROUP = N_Q_HEADS // N_KV_HEADS
HEAD_DIM = 128
GRID_W = 64
WINDOW = 128
ROPE_THETA = 10000.0
ROPE_AXIS_DIM = HEAD_DIM // 2
ROPE_PAIRS = ROPE_AXIS_DIM // 2
N_SUB = 3
EPS = 1e-6
NEG = -1e30
ATTN_SCALE = HEAD_DIM ** -0.5

VMEM_LIMIT_BYTES = 60 * 1024 * 1024
MXU_COLS = 256
ROW_TILE = 256
MM_ROWS = 1024


def _params(*sem):
    return pltpu.CompilerParams(dimension_semantics=sem, vmem_limit_bytes=VMEM_LIMIT_BYTES)


def _rms(x, g):
    return x * lax.rsqrt(jnp.mean(x * x, axis=-1, keepdims=True) + EPS) * g


def _mod_tile(c_ref, w_ref, b_ref, o_ref):
    x = c_ref[...]
    x = (x * jax.nn.sigmoid(x)).astype(BF16)
    o_ref[...] = jnp.dot(x, w_ref[...].astype(BF16), preferred_element_type=F32) + b_ref[...]


def _modulation(cond, w_mod, b_mod2, n_cols):
    rows, d = cond.shape
    tn = 512
    return pl.pallas_call(
        _mod_tile,
        grid=(n_cols // tn,),
        in_specs=[pl.BlockSpec((rows, d), lambda j: (0, 0)),
                  pl.BlockSpec((d, tn), lambda j: (0, j)),
                  pl.BlockSpec((1, tn), lambda j: (0, j))],
        out_specs=pl.BlockSpec((rows, tn), lambda j: (0, j)),
        out_shape=jax.ShapeDtypeStruct((rows, n_cols), F32),
        compiler_params=_params("arbitrary"),
        name="modulation",
    )(cond, w_mod, b_mod2)


def _with_rows(h_refs, n_ctx_tiles, body):
    if len(h_refs) == 1:
        body(h_refs[0])
        return
    for k, h_ref in enumerate(h_refs):
        @pl.when((pl.program_id(0) < n_ctx_tiles) == (k == 0))
        def _():
            body(h_ref)


def _pre_kernel(s, n_h, n_ctx_tiles, *refs):
    h_refs, (m_ref, gpre_ref, u_ref) = refs[:n_h], refs[n_h:]

    def body(h_ref):
        shift = m_ref[pl.ds(3 * s, 1), :]
        scale = m_ref[pl.ds(3 * s + 1, 1), :]
        u = _rms(h_ref[...], gpre_ref[pl.ds(s, 1), :]) * (1.0 + scale) + shift
        u_ref[...] = u.astype(u_ref.dtype)

    _with_rows(h_refs, n_ctx_tiles, body)


def _post_kernel(s, res_w, with_pre, n_h, n_ctx_tiles, *refs):
    h_refs, (o_ref, m_ref, gpost_ref, gpre_ref, hn_ref, *u_ref) = refs[:n_h], refs[n_h:]

    def body(h_ref):
        gate = m_ref[pl.ds(3 * s + 2, 1), :]
        hn = h_ref[...] + (res_w * gate) * _rms(o_ref[...], gpost_ref[pl.ds(s, 1), :])
        hn_ref[...] = hn
        if with_pre:
            shift = m_ref[pl.ds(3 * (s + 1), 1), :]
            scale = m_ref[pl.ds(3 * (s + 1) + 1, 1), :]
            u = _rms(hn, gpre_ref[pl.ds(s + 1, 1), :]) * (1.0 + scale) + shift
            u_ref[0][...] = u.astype(BF16)

    _with_rows(h_refs, n_ctx_tiles, body)


def _group_of_tile(i, tile_rows, n_ctx, lat_seq):
    r0 = i * tile_rows
    return jnp.where(r0 < n_ctx, 0, 1 + (r0 - n_ctx) // lat_seq)


def _row_specs(hs, d, t0=0):
    if len(hs) == 1:
        return [pl.BlockSpec((ROW_TILE, d), lambda i: (i + t0, 0))], 0
    t_ctx, t_lat = (x.shape[0] // ROW_TILE for x in hs)
    return [pl.BlockSpec((ROW_TILE, d), lambda i: (jnp.minimum(i, t_ctx - 1), 0)),
            pl.BlockSpec((ROW_TILE, d), lambda i: (jnp.clip(i - t_ctx, 0, t_lat - 1), 0))], t_ctx


def _pre(s, hs, mod3, g_pre, n_ctx, lat_seq):
    n = sum(x.shape[0] for x in hs)
    d = hs[0].shape[1]
    grp = functools.partial(_group_of_tile, tile_rows=ROW_TILE, n_ctx=n_ctx, lat_seq=lat_seq)
    h_specs, t_ctx = _row_specs(hs, d)
    return pl.pallas_call(
        functools.partial(_pre_kernel, s, len(hs), t_ctx),
        grid=(n // ROW_TILE,),
        in_specs=h_specs + [pl.BlockSpec((None, mod3.shape[1], d), lambda i: (grp(i), 0, 0)),
                            pl.BlockSpec((N_SUB, d), lambda i: (0, 0))],
        out_specs=pl.BlockSpec((ROW_TILE, d), lambda i: (i, 0)),
        out_shape=jax.ShapeDtypeStruct((n, d), BF16),
        compiler_params=_params("parallel"),
        name=f"pre{s}",
    )(*hs, mod3, g_pre)


def _post(s, res_w, hs, o, mod3, g_post, g_pre, n_ctx, lat_seq, with_pre, row0=0, rows=None):
    d = o.shape[1]
    rows = o.shape[0] if rows is None else rows
    t0 = row0 // ROW_TILE
    grp = lambda i: _group_of_tile(i + t0, ROW_TILE, n_ctx, lat_seq)
    h_specs, t_ctx = _row_specs(hs, d, t0)
    out_shape = [jax.ShapeDtypeStruct((rows, d), F32)]
    out_specs = [pl.BlockSpec((ROW_TILE, d), lambda i: (i, 0))]
    if with_pre:
        out_shape.append(jax.ShapeDtypeStruct((rows, d), BF16))
        out_specs.append(pl.BlockSpec((ROW_TILE, d), lambda i: (i, 0)))
    return pl.pallas_call(
        functools.partial(_post_kernel, s, res_w, with_pre, len(hs), t_ctx),
        grid=(rows // ROW_TILE,),
        in_specs=h_specs + [pl.BlockSpec((ROW_TILE, d), lambda i: (i + t0, 0)),
                            pl.BlockSpec((None, 3 * N_SUB, d), lambda i: (grp(i), 0, 0)),
                            pl.BlockSpec((N_SUB, d), lambda i: (0, 0)),
                            pl.BlockSpec((N_SUB, d), lambda i: (0, 0))],
        out_specs=out_specs,
        out_shape=out_shape,
        compiler_params=_params("parallel"),
        name=f"post{s}",
    )(*hs, o, mod3, g_post, g_pre)


def _mm_kernel(slabs, n_w, n_out, combine, *refs):
    n_x = sum(n for n, _ in slabs)
    x_refs, refs = refs[:n_x], refs[n_x:]
    w_refs, o_refs, wb_refs = refs[:n_w], refs[n_w:n_w + n_out], refs[n_w + n_out:]

    @pl.when(pl.program_id(1) == 0)
    def _():
        for w_ref, wb_ref in zip(w_refs, wb_refs):
            wb_ref[...] = w_ref[...].astype(BF16)

    def compute(slab_refs):
        ys = []
        for wb_ref in wb_refs:
            k0, y = 0, None
            for x_ref in slab_refs:
                k1 = k0 + x_ref.shape[1]
                part = jnp.dot(x_ref[...], wb_ref[k0:k1, :], preferred_element_type=F32)
                y = part if y is None else y + part
                k0 = k1
            ys.append(y)
        for o_ref, y in zip(o_refs, combine(*ys)):
            o_ref[...] = y.astype(o_ref.dtype)

    starts = np.cumsum([0] + [n for n, _ in slabs])
    split_at = {t for n, t in slabs if n == 2}
    if not split_at:
        compute(x_refs)
    else:
        (t_first,) = split_at
        for is_first in (True, False):
            @pl.when((pl.program_id(1) < t_first) == is_first)
            def _():
                compute([x_refs[s + (1 if n == 2 and not is_first else 0)] for s, (n, _) in zip(starts, slabs)])


def _matmul(xs, ws, combine, out_dtypes, n_cols, tn, name):
    x_specs, x_args, slabs = [], [], []
    for x in xs:
        if isinstance(x, tuple):
            t_a, t_b = (p.shape[0] // MM_ROWS for p in x)
            x_specs += [pl.BlockSpec((MM_ROWS, x[0].shape[1]), lambda j, i, t=t_a: (jnp.minimum(i, t - 1), 0)),
                        pl.BlockSpec((MM_ROWS, x[1].shape[1]),
                                     lambda j, i, t=t_a, u=t_b: (jnp.clip(i - t, 0, u - 1), 0))]
            x_args += list(x)
            slabs.append((2, t_a))
        else:
            x_specs.append(pl.BlockSpec((MM_ROWS, x.shape[1]), lambda j, i: (i, 0)))
            x_args.append(x)
            slabs.append((1, 0))
    m = sum(p.shape[0] for p in xs[0]) if isinstance(xs[0], tuple) else xs[0].shape[0]
    k = sum((x[0] if isinstance(x, tuple) else x).shape[1] for x in xs)
    w_specs = [pl.BlockSpec((k, tn), lambda j, i, c0=col0 // tn: (0, j + c0)) for _, col0 in ws]
    return pl.pallas_call(
        functools.partial(_mm_kernel, tuple(slabs), len(ws), len(out_dtypes), combine),
        grid=(n_cols // tn, m // MM_ROWS),
        in_specs=x_specs + w_specs,
        out_specs=[pl.BlockSpec((MM_ROWS, tn), lambda j, i: (i, j))] * len(out_dtypes),
        out_shape=[jax.ShapeDtypeStruct((m, n_cols), dt) for dt in out_dtypes],
        scratch_shapes=[pltpu.VMEM((k, tn), BF16)] * len(ws),
        compiler_params=_params("arbitrary", "arbitrary"),
        name=name,
    )(*x_args, *(w for w, _ in ws))


def _swiglu_combine(g, u):
    return ((g * jax.nn.sigmoid(g)) * u,)


def _identity_combine(y):
    return (y,)


def _conv_gate_combine(gb, gc, hc):
    return gb, gc * hc


def _ffn_up_mod_kernel(n_side, x_ref, wg_ref, wu_ref, c_ref, wm_ref, b_ref, a_ref, m_ref, wgb_ref, wub_ref):
    _mm_kernel(((1, 0),), 2, 1, _swiglu_combine, x_ref, wg_ref, wu_ref, a_ref, wgb_ref, wub_ref)

    @pl.when(pl.program_id(0) * pl.num_programs(1) + pl.program_id(1) < n_side)
    def _():
        _mod_tile(c_ref, wm_ref, b_ref, m_ref)


def _ffn_up_with_modulation(x, wg, wu, cond, w_mod, b_mod2, col0):
    m, k = x.shape
    f = wg.shape[1]
    tn = MXU_COLS
    tmod = 2 * MXU_COLS
    n_i = m // MM_ROWS
    n_side = (w_mod.shape[1] - col0) // tmod
    assert n_side <= (f // tn) * n_i
    side = lambda j, i: (0, col0 // tmod + jnp.minimum(j * n_i + i, n_side - 1))
    return pl.pallas_call(
        functools.partial(_ffn_up_mod_kernel, n_side),
        grid=(f // tn, n_i),
        in_specs=[pl.BlockSpec((MM_ROWS, k), lambda j, i: (i, 0)),
                  pl.BlockSpec((k, tn), lambda j, i: (0, j)),
                  pl.BlockSpec((k, tn), lambda j, i: (0, j)),
                  pl.BlockSpec(cond.shape, lambda j, i: (0, 0)),
                  pl.BlockSpec((w_mod.shape[0], tmod), side),
                  pl.BlockSpec((1, tmod), side)],
        out_specs=[pl.BlockSpec((MM_ROWS, tn), lambda j, i: (i, j)),
                   pl.BlockSpec((cond.shape[0], tmod), lambda j, i: (0, jnp.minimum(j * n_i + i, n_side - 1)))],
        out_shape=[jax.ShapeDtypeStruct((m, f), BF16),
                   jax.ShapeDtypeStruct((cond.shape[0], n_side * tmod), F32)],
        scratch_shapes=[pltpu.VMEM((k, tn), BF16)] * 2,
        compiler_params=_params("arbitrary", "arbitrary"),
        name="ffn_up_mod",
    )(x, wg, wu, cond, w_mod, b_mod2)


def _down_kernel(a_ref, w_ref, o_ref):
    o_ref[...] = jnp.dot(a_ref[...], w_ref[...].astype(BF16), preferred_element_type=F32)


def _down(a, w):
    m, f = a.shape
    d = w.shape[1]
    tn = MXU_COLS
    return pl.pallas_call(
        _down_kernel,
        grid=(m // MM_ROWS, d // tn),
        in_specs=[pl.BlockSpec((MM_ROWS, f), lambda i, j: (i, 0), pipeline_mode=pl.Buffered(1)),
                  pl.BlockSpec((f, tn), lambda i, j: (0, j))],
        out_specs=pl.BlockSpec((MM_ROWS, tn), lambda i, j: (i, j)),
        out_shape=jax.ShapeDtypeStruct((m, d), F32),
        compiler_params=_params("parallel", "arbitrary"),
        name="ffn_down",
    )(a, w)


def _softmax_with_sink(s, sink):
    m = jnp.maximum(jnp.max(s, axis=-1, keepdims=True), sink)
    p = jnp.exp(s - m)
    denom = jnp.sum(p, axis=-1, keepdims=True) + jnp.exp(sink - m)
    return p * (1.0 / denom)


def _qk(q, k):
    return lax.dot_general(q, k, (((1,), (1,)), ((), ())), preferred_element_type=F32) * ATTN_SCALE


def _group_sinks(sink_ref, h, rows_per_head):
    head = lax.broadcasted_iota(jnp.int32, (GQA_GROUP * rows_per_head, 1), 0) // rows_per_head
    col = jnp.full(head.shape, sink_ref[h * GQA_GROUP], F32)
    for g in range(1, GQA_GROUP):
        col = jnp.where(head == g, sink_ref[h * GQA_GROUP + g], col)
    return col


def _stack_heads(q_ref, prep):
    return jnp.concatenate([prep(q_ref[:, g * HEAD_DIM:(g + 1) * HEAD_DIM]).astype(BF16)
                            for g in range(GQA_GROUP)], axis=0)


def _unstack_heads(o, o_ref):
    rows = o_ref.shape[0]
    for g in range(GQA_GROUP):
        o_ref[:, g * HEAD_DIM:(g + 1) * HEAD_DIM] = o[g * rows:(g + 1) * rows, :]


def _ctx_attn_kernel(sink_ref, q_ref, k_ref, v_ref, g_ref, o_ref, ks_ref, vs_ref, acc_ref):
    ks_ref[...] = k_ref[...]
    vs_ref[...] = v_ref[...]
    qw = GQA_GROUP * HEAD_DIM
    for h in range(N_KV_HEADS):
        kv_cols = slice(h * HEAD_DIM, (h + 1) * HEAD_DIM)
        q = _stack_heads(q_ref.at[:, h * qw:(h + 1) * qw], lambda x: x)
        s = _qk(q, k_ref[:, kv_cols].astype(BF16))
        p = _softmax_with_sink(s, _group_sinks(sink_ref, h, q_ref.shape[0]))
        o = jnp.dot(p.astype(BF16), v_ref[:, kv_cols].astype(BF16), preferred_element_type=F32)
        _unstack_heads(o, acc_ref.at[:, h * qw:(h + 1) * qw])
    o_ref[...] = _rms(acc_ref[...], g_ref[...]).astype(o_ref.dtype)


def _ctx_attention(z, sink, g_a, n_batch, seq):
    q_w = N_Q_HEADS * HEAD_DIM
    kv_w = N_KV_HEADS * HEAD_DIM
    state = jax.ShapeDtypeStruct((n_batch, seq, kv_w), F32)
    return pl.pallas_call(
        _ctx_attn_kernel,
        grid=(n_batch,),
        in_specs=[pl.BlockSpec(memory_space=pltpu.SMEM),
                  pl.BlockSpec((seq, q_w), lambda b: (b, 0)),
                  pl.BlockSpec((seq, kv_w), lambda b: (b, q_w // kv_w)),
                  pl.BlockSpec((seq, kv_w), lambda b: (b, q_w // kv_w + 1)),
                  pl.BlockSpec((1, q_w), lambda b: (0, 0))],
        out_specs=[pl.BlockSpec((seq, q_w), lambda b: (b, 0)),
                   pl.BlockSpec((None, seq, kv_w), lambda b: (b, 0, 0)),
                   pl.BlockSpec((None, seq, kv_w), lambda b: (b, 0, 0))],
        out_shape=[jax.ShapeDtypeStruct((n_batch * seq, q_w), BF16), state, state],
        scratch_shapes=[pltpu.VMEM((seq, q_w), F32)],
        compiler_params=_params("parallel"),
        name="ctx_attention",
    )(sink, z, z, z, g_a.reshape(1, q_w))


def _rope(x, cos, sin_signed):
    lane = lax.broadcasted_iota(jnp.int32, x.shape, 1)
    first = (lane % ROPE_AXIS_DIM) < ROPE_PAIRS
    partner = jnp.where(first, pltpu.roll(x, HEAD_DIM - ROPE_PAIRS, axis=1),
                        pltpu.roll(x, ROPE_PAIRS, axis=1))
    return x * cos + partner * sin_signed


def _lat_attn_kernel(sink_ref, q_ref, k_ref, v_ref, kc_ref, vc_ref, cos_ref, sin_ref, g_ref, o_ref,
                     kr_ref, vb_ref, kcb_ref, vcb_ref, acc_ref):
    i = pl.program_id(1)
    n_blk = pl.num_programs(1)
    blk = q_ref.shape[0]
    qw = GQA_GROUP * HEAD_DIM

    @pl.when(i == 0)
    def _():
        for h in range(N_KV_HEADS):
            kv_cols = slice(h * HEAD_DIM, (h + 1) * HEAD_DIM)
            kr_ref[:, kv_cols] = _rope(k_ref[:, kv_cols], cos_ref[...], sin_ref[...]).astype(BF16)
        vb_ref[...] = v_ref[...].astype(BF16)
        kcb_ref[...] = kc_ref[...].astype(BF16)
        vcb_ref[...] = vc_ref[...].astype(BF16)

    q0 = pl.multiple_of(i * blk, blk)
    w0 = pl.multiple_of(jnp.clip(i - 1, 0, n_blk - 3) * blk, blk)
    cos = cos_ref[pl.ds(q0, blk), :]
    sin = sin_ref[pl.ds(q0, blk), :]
    rows = GQA_GROUP * blk
    q_pos = q0 + lax.broadcasted_iota(jnp.int32, (rows, 3 * blk), 0) % blk
    k_pos = w0 + lax.broadcasted_iota(jnp.int32, (rows, 3 * blk), 1)
    in_window = jnp.abs(k_pos - q_pos) <= WINDOW
    for h in range(N_KV_HEADS):
        kv_cols = slice(h * HEAD_DIM, (h + 1) * HEAD_DIM)
        q = _stack_heads(q_ref.at[:, h * qw:(h + 1) * qw], lambda x: _rope(x, cos, sin))
        s = jnp.concatenate([jnp.where(in_window, _qk(q, kr_ref[pl.ds(w0, 3 * blk), kv_cols]), NEG),
                             _qk(q, kcb_ref[:, kv_cols])], axis=-1)
        p = _softmax_with_sink(s, _group_sinks(sink_ref, h, blk)).astype(BF16)
        o = (jnp.dot(p[:, :3 * blk], vb_ref[pl.ds(w0, 3 * blk), kv_cols], preferred_element_type=F32)
             + jnp.dot(p[:, 3 * blk:], vcb_ref[:, kv_cols], preferred_element_type=F32))
        _unstack_heads(o, acc_ref.at[:, h * qw:(h + 1) * qw])
    o_ref[...] = _rms(acc_ref[...], g_ref[...]).astype(o_ref.dtype)


def _lat_attention(z, cache_k, cache_v, sink, g_a, cos, sin_signed, row0, n_batch, seq):
    blk = WINDOW
    n_blk = seq // blk
    past = cache_k.shape[1]
    q_w = N_Q_HEADS * HEAD_DIM
    kv_w = N_KV_HEADS * HEAD_DIM
    return pl.pallas_call(
        _lat_attn_kernel,
        grid=(n_batch, n_blk),
        in_specs=[pl.BlockSpec(memory_space=pltpu.SMEM),
                  pl.BlockSpec((blk, q_w), lambda b, i: (row0 // blk + b * n_blk + i, 0)),
                  pl.BlockSpec((seq, kv_w), lambda b, i: (row0 // seq + b, q_w // kv_w)),
                  pl.BlockSpec((seq, kv_w), lambda b, i: (row0 // seq + b, q_w // kv_w + 1)),
                  pl.BlockSpec((None, past, kv_w), lambda b, i: (b, 0, 0)),
                  pl.BlockSpec((None, past, kv_w), lambda b, i: (b, 0, 0)),
                  pl.BlockSpec((seq, HEAD_DIM), lambda b, i: (0, 0)),
                  pl.BlockSpec((seq, HEAD_DIM), lambda b, i: (0, 0)),
                  pl.BlockSpec((1, q_w), lambda b, i: (0, 0))],
        out_specs=pl.BlockSpec((blk, q_w), lambda b, i: (b * n_blk + i, 0)),
        out_shape=jax.ShapeDtypeStruct((n_batch * seq, q_w), BF16),
        scratch_shapes=[pltpu.VMEM((seq, kv_w), BF16), pltpu.VMEM((seq, kv_w), BF16),
                        pltpu.VMEM((past, kv_w), BF16), pltpu.VMEM((past, kv_w), BF16),
                        pltpu.VMEM((blk, q_w), F32)],
        compiler_params=_params("parallel", "arbitrary"),
        name="latent_attention",
    )(sink, z, z, z, cache_k, cache_v, cos, sin_signed, g_a.reshape(1, q_w))


def _rope_tables(seq):
    t = np.arange(seq)
    pos = np.stack([t // GRID_W, t % GRID_W], axis=1).astype(np.float64)
    inv_freq = ROPE_THETA ** (-np.arange(ROPE_PAIRS, dtype=np.float64) / ROPE_PAIRS)
    ang = pos[:, :, None] * inv_freq[None, None, :]
    cos, sin = np.cos(ang), np.sin(ang)
    cos = np.concatenate([cos, cos], axis=-1).reshape(seq, HEAD_DIM)
    sin = np.concatenate([-sin, sin], axis=-1).reshape(seq, HEAD_DIM)
    return jnp.asarray(cos, F32), jnp.asarray(sin, F32)


def _conv_norm_kernel(n_ctx, lat_seq, gb_ref, u_ref, up_ref, un_ref, w_ref, gc_ref, o_ref):
    t = pl.program_id(0)
    rows = u_ref.shape[0]
    r0 = t * rows
    is_ctx = r0 < n_ctx
    is_first = jnp.logical_or(is_ctx, (r0 - n_ctx) % lat_seq == 0)
    is_last = jnp.logical_or(is_ctx, (r0 + rows - n_ctx) % lat_seq == 0)
    u = u_ref[...]
    u_before = jnp.where(is_first, 0.0, up_ref[pl.ds(up_ref.shape[0] - 1, 1), :])
    u_after = jnp.where(is_last, 0.0, un_ref[pl.ds(0, 1), :])
    row = lax.broadcasted_iota(jnp.int32, u.shape, 0)
    u_prev = jnp.where(row == 0, u_before, pltpu.roll(u, 1, axis=0))
    u_next = jnp.where(row == rows - 1, u_after, pltpu.roll(u, rows - 1, axis=0))
    y = u_prev * w_ref[pl.ds(0, 1), :] + u * w_ref[pl.ds(1, 1), :] + u_next * w_ref[pl.ds(2, 1), :]
    o_ref[...] = _rms(gb_ref[...] * y, gc_ref[...]).astype(o_ref.dtype)


def _conv_norm(gb, u, w_conv, g_c, n_ctx, ctx_seq, lat_seq):
    assert ctx_seq == ROW_TILE
    n, wc = u.shape
    halo = 8
    per = ROW_TILE // halo
    last_halo = n // halo - 1
    return pl.pallas_call(
        functools.partial(_conv_norm_kernel, n_ctx, lat_seq),
        grid=(n // ROW_TILE,),
        in_specs=[pl.BlockSpec((ROW_TILE, wc), lambda t: (t, 0)),
                  pl.BlockSpec((ROW_TILE, wc), lambda t: (t, 0)),
                  pl.BlockSpec((halo, wc), lambda t: (jnp.maximum(t * per - 1, 0), 0)),
                  pl.BlockSpec((halo, wc), lambda t: (jnp.minimum((t + 1) * per, last_halo), 0)),
                  pl.BlockSpec((w_conv.shape[0], wc), lambda t: (0, 0)),
                  pl.BlockSpec((1, wc), lambda t: (0, 0))],
        out_specs=pl.BlockSpec((ROW_TILE, wc), lambda t: (t, 0)),
        out_shape=jax.ShapeDtypeStruct((n, wc), BF16),
        compiler_params=_params("parallel"),
        name="conv_norm",
    )(gb, u, u, u, w_conv, g_c.reshape(1, wc))


def kernel(x_prompt, x_sample, c, cache_k, cache_v, c_ctx, w_mod, b_mod, g_pre, g_post, w_in, w_conv,
           sink, g_attn_out, g_conv_out, w_o, w_ffn1_gate, w_ffn1_up, w_ffn1_down,
           w_ffn2_gate, w_ffn2_up, w_ffn2_down):
    batch, seq, d = x_prompt.shape
    dec_batch, dec_seq, _ = x_sample.shape
    depth = w_mod.shape[0]
    past = cache_k.shape[2]
    n_ctx = batch * seq
    n_lat = dec_batch * dec_seq
    attn_w = N_Q_HEADS * HEAD_DIM
    kv_w = N_KV_HEADS * HEAD_DIM
    conv_w = d - attn_w
    d_ff = w_ffn1_gate.shape[2]

    hs = (x_prompt.reshape(n_ctx, d), x_sample.reshape(n_lat, d))
    cond = jnp.concatenate([c_ctx[None, :], c, jnp.zeros((8 - 1 - dec_batch, d), F32)], axis=0)
    cos, sin_signed = _rope_tables(dec_seq)
    post = functools.partial(_post, n_ctx=n_ctx, lat_seq=dec_seq)
    wide = 2 * MXU_COLS

    ks_new, vs_new = [], []
    for l in range(depth):
        b_mod2 = b_mod[l].reshape(1, -1)
        mod_head = _modulation(cond, w_mod[l], b_mod2, 2 * d)
        u = _pre(0, hs, mod_head.reshape(8, 2, d), g_pre[l], n_ctx, dec_seq)
        a, mod_tail = _ffn_up_with_modulation(u, w_ffn1_gate[l], w_ffn1_up[l], cond, w_mod[l], b_mod2, 2 * d)
        mod3 = jnp.concatenate([mod_head, mod_tail], axis=1).reshape(8, 3 * N_SUB, d)
        o = _down(a, w_ffn1_down[l])
        h, u = post(0, 0.5, hs, o, mod3, g_post[l], g_pre[l], with_pre=True)

        qkv_w = attn_w + 2 * kv_w
        (zqkv,) = _matmul([u], [(w_in[l], 0)], _identity_combine, [F32], qkv_w, wide, "in_proj_qkv")
        gb, uc = _matmul([u], [(w_in[l], qkv_w), (w_in[l], qkv_w + conv_w), (w_in[l], qkv_w + 2 * conv_w)],
                         _conv_gate_combine, [F32, F32], conv_w, MXU_COLS, "in_proj_conv")
        attn_ctx, k_new, v_new = _ctx_attention(zqkv, sink[l], g_attn_out[l], batch, seq)
        attn_lat = _lat_attention(zqkv, cache_k[:, l].reshape(dec_batch, past, kv_w),
                                  cache_v[:, l].reshape(dec_batch, past, kv_w),
                                  sink[l], g_attn_out[l], cos, sin_signed, n_ctx, dec_batch, dec_seq)
        conv_n = _conv_norm(gb, uc, w_conv[l], g_conv_out[l], n_ctx, seq, dec_seq)
        (o,) = _matmul([(attn_ctx, attn_lat), conv_n], [(w_o[l], 0)], _identity_combine, [F32], d, wide,
                       "out_proj")
        h, u = post(1, 1.0, (h,), o, mod3, g_post[l], g_pre[l], with_pre=True)

        (a,) = _matmul([u], [(w_ffn2_gate[l], 0), (w_ffn2_up[l], 0)], _swiglu_combine, [BF16], d_ff,
                       MXU_COLS, "ffn_up")
        o = _down(a, w_ffn2_down[l])
        if l + 1 < depth:
            (h,) = post(2, 0.5, (h,), o, mod3, g_post[l], g_pre[l], with_pre=False)
            hs = (h,)
        ks_new.append(k_new.reshape(batch, seq, N_KV_HEADS, HEAD_DIM))
        vs_new.append(v_new.reshape(batch, seq, N_KV_HEADS, HEAD_DIM))

    (y_prompt,) = post(2, 0.5, (h,), o, mod3, g_post[depth - 1], g_pre[depth - 1], with_pre=False,
                       row0=0, rows=n_ctx)
    (y_sample,) = post(2, 0.5, (h,), o, mod3, g_post[depth - 1], g_pre[depth - 1], with_pre=False,
                       row0=n_ctx, rows=n_lat)
    return (y_prompt.reshape(batch, seq, d), y_sample.reshape(dec_batch, dec_seq, d),
            jnp.stack(ks_new, axis=1), jnp.stack(vs_new, axis=1))
```

```python
import functools

import jax
import jax.numpy as jnp
import numpy as np
from jax import lax
from jax.experimental import pallas as pl
from jax.experimental.pallas import tpu as pltpu

F32 = jnp.float32
BF16 = jnp.bfloat16

N_Q_HEADS = 16
N_KV_HEADS = 4
GQA_GROUP = N_Q_HEADS // N_KV_HEADS
HEAD_DIM = 128
GRID_W = 64
WINDOW = 128
ROPE_THETA = 10000.0
ROPE_AXIS_DIM = HEAD_DIM // 2
ROPE_PAIRS = ROPE_AXIS_DIM // 2
N_SUB = 3
EPS = 1e-6
NEG = -1e30
ATTN_SCALE = HEAD_DIM ** -0.5

VMEM_LIMIT_BYTES = 60 * 1024 * 1024
MXU_COLS = 256
ROW_TILE = 256
MM_ROWS = 1024


def _params(*sem):
    return pltpu.CompilerParams(dimension_semantics=sem, vmem_limit_bytes=VMEM_LIMIT_BYTES)


def _rms(x, g):
    return x * lax.rsqrt(jnp.mean(x * x, axis=-1, keepdims=True) + EPS) * g


def _mod_tile(c_ref, w_ref, b_ref, o_ref):
    x = c_ref[...]
    x = (x * jax.nn.sigmoid(x)).astype(BF16)
    o_ref[...] = jnp.dot(x, w_ref[...].astype(BF16), preferred_element_type=F32) + b_ref[...]


def _modulation(cond, w_mod, b_mod2, n_cols):
    rows, d = cond.shape
    tn = 512
    return pl.pallas_call(
        _mod_tile,
        grid=(n_cols // tn,),
        in_specs=[pl.BlockSpec((rows, d), lambda j: (0, 0)),
                  pl.BlockSpec((d, tn), lambda j: (0, j)),
                  pl.BlockSpec((1, tn), lambda j: (0, j))],
        out_specs=pl.BlockSpec((rows, tn), lambda j: (0, j)),
        out_shape=jax.ShapeDtypeStruct((rows, n_cols), F32),
        compiler_params=_params("arbitrary"),
        name="modulation",
    )(cond, w_mod, b_mod2)


def _load_rows(h_refs, n_ctx_tiles):
    if len(h_refs) == 1:
        return h_refs[0][...]
    return jnp.where(pl.program_id(0) < n_ctx_tiles, h_refs[0][...], h_refs[1][...])


def _pre_kernel(s, n_h, n_ctx_tiles, *refs):
    h_refs, (m_ref, gpre_ref, u_ref) = refs[:n_h], refs[n_h:]
    shift = m_ref[pl.ds(3 * s, 1), :]
    scale = m_ref[pl.ds(3 * s + 1, 1), :]
    u = _rms(_load_rows(h_refs, n_ctx_tiles), gpre_ref[pl.ds(s, 1), :]) * (1.0 + scale) + shift
    u_ref[...] = u.astype(u_ref.dtype)


def _post_kernel(s, res_w, with_pre, n_h, n_ctx_tiles, *refs):
    h_refs, (o_ref, m_ref, gpost_ref, gpre_ref, hn_ref, *u_ref) = refs[:n_h], refs[n_h:]
    gate = m_ref[pl.ds(3 * s + 2, 1), :]
    hn = _load_rows(h_refs, n_ctx_tiles) + (res_w * gate) * _rms(o_ref[...], gpost_ref[pl.ds(s, 1), :])
    hn_ref[...] = hn
    if with_pre:
        shift = m_ref[pl.ds(3 * (s + 1), 1), :]
        scale = m_ref[pl.ds(3 * (s + 1) + 1, 1), :]
        u = _rms(hn, gpre_ref[pl.ds(s + 1, 1), :]) * (1.0 + scale) + shift
        u_ref[0][...] = u.astype(BF16)


def _group_of_tile(i, tile_rows, n_ctx, lat_seq):
    r0 = i * tile_rows
    return jnp.where(r0 < n_ctx, 0, 1 + (r0 - n_ctx) // lat_seq)


def _row_specs(hs, d, t0=0):
    if len(hs) == 1:
        return [pl.BlockSpec((ROW_TILE, d), lambda i: (i + t0, 0))], 0
    t_ctx, t_lat = (x.shape[0] // ROW_TILE for x in hs)
    return [pl.BlockSpec((ROW_TILE, d), lambda i: (jnp.minimum(i, t_ctx - 1), 0)),
            pl.BlockSpec((ROW_TILE, d), lambda i: (jnp.clip(i - t_ctx, 0, t_lat - 1), 0))], t_ctx


def _pre(s, hs, mod3, g_pre, n_ctx, lat_seq):
    n = sum(x.shape[0] for x in hs)
    d = hs[0].shape[1]
    grp = functools.partial(_group_of_tile, tile_rows=ROW_TILE, n_ctx=n_ctx, lat_seq=lat_seq)
    h_specs, t_ctx = _row_specs(hs, d)
    return pl.pallas_call(
        functools.partial(_pre_kernel, s, len(hs), t_ctx),
        grid=(n // ROW_TILE,),
        in_specs=h_specs + [pl.BlockSpec((None, mod3.shape[1], d), lambda i: (grp(i), 0, 0)),
                            pl.BlockSpec((N_SUB, d), lambda i: (0, 0))],
        out_specs=pl.BlockSpec((ROW_TILE, d), lambda i: (i, 0)),
        out_shape=jax.ShapeDtypeStruct((n, d), BF16),
        compiler_params=_params("parallel"),
        name=f"pre{s}",
    )(*hs, mod3, g_pre)


def _post(s, res_w, hs, o, mod3, g_post, g_pre, n_ctx, lat_seq, with_pre, row0=0, rows=None):
    d = o.shape[1]
    rows = o.shape[0] if rows is None else rows
    t0 = row0 // ROW_TILE
    grp = lambda i: _group_of_tile(i + t0, ROW_TILE, n_ctx, lat_seq)
    h_specs, t_ctx = _row_specs(hs, d, t0)
    out_shape = [jax.ShapeDtypeStruct((rows, d), F32)]
    out_specs = [pl.BlockSpec((ROW_TILE, d), lambda i: (i, 0))]
    if with_pre:
        out_shape.append(jax.ShapeDtypeStruct((rows, d), BF16))
        out_specs.append(pl.BlockSpec((ROW_TILE, d), lambda i: (i, 0)))
    return pl.pallas_call(
        functools.partial(_post_kernel, s, res_w, with_pre, len(hs), t_ctx),
        grid=(rows // ROW_TILE,),
        in_specs=h_specs + [pl.BlockSpec((ROW_TILE, d), lambda i: (i + t0, 0)),
                            pl.BlockSpec((None, 3 * N_SUB, d), lambda i: (grp(i), 0, 0)),
                            pl.BlockSpec((N_SUB, d), lambda i: (0, 0)),
                            pl.BlockSpec((N_SUB, d), lambda i: (0, 0))],
        out_specs=out_specs,
        out_shape=out_shape,
        compiler_params=_params("parallel"),
        name=f"post{s}",
    )(*hs, o, mod3, g_post, g_pre)


def _mm_kernel(slabs, n_w, n_out, combine, *refs):
    n_x = sum(n for n, _ in slabs)
    x_refs, refs = refs[:n_x], refs[n_x:]
    w_refs, o_refs, wb_refs = refs[:n_w], refs[n_w:n_w + n_out], refs[n_w + n_out:]

    @pl.when(pl.program_id(1) == 0)
    def _():
        for w_ref, wb_ref in zip(w_refs, wb_refs):
            wb_ref[...] = w_ref[...].astype(BF16)

    xs, r = [], 0
    for n, t_first in slabs:
        if n == 1:
            xs.append(x_refs[r][...])
        else:
            xs.append(jnp.where(pl.program_id(1) < t_first, x_refs[r][...], x_refs[r + 1][...]))
        r += n
    ys = []
    for wb_ref in wb_refs:
        k0, y = 0, None
        for x in xs:
            k1 = k0 + x.shape[1]
            part = jnp.dot(x, wb_ref[k0:k1, :], preferred_element_type=F32)
            y = part if y is None else y + part
            k0 = k1
        ys.append(y)
    for o_ref, y in zip(o_refs, combine(*ys)):
        o_ref[...] = y.astype(o_ref.dtype)


def _matmul(xs, ws, combine, out_dtypes, n_cols, tn, name):
    x_specs, x_args, slabs = [], [], []
    for x in xs:
        if isinstance(x, tuple):
            t_a, t_b = (p.shape[0] // MM_ROWS for p in x)
            x_specs += [pl.BlockSpec((MM_ROWS, x[0].shape[1]), lambda j, i, t=t_a: (jnp.minimum(i, t - 1), 0)),
                        pl.BlockSpec((MM_ROWS, x[1].shape[1]),
                                     lambda j, i, t=t_a, u=t_b: (jnp.clip(i - t, 0, u - 1), 0))]
            x_args += list(x)
            slabs.append((2, t_a))
        else:
            x_specs.append(pl.BlockSpec((MM_ROWS, x.shape[1]), lambda j, i: (i, 0)))
            x_args.append(x)
            slabs.append((1, 0))
    m = sum(p.shape[0] for p in xs[0]) if isinstance(xs[0], tuple) else xs[0].shape[0]
    k = sum((x[0] if isinstance(x, tuple) else x).shape[1] for x in xs)
    w_specs = [pl.BlockSpec((k, tn), lambda j, i, c0=col0 // tn: (0, j + c0)) for _, col0 in ws]
    return pl.pallas_call(
        functools.partial(_mm_kernel, tuple(slabs), len(ws), len(out_dtypes), combine),
        grid=(n_cols // tn, m // MM_ROWS),
        in_specs=x_specs + w_specs,
        out_specs=[pl.BlockSpec((MM_ROWS, tn), lambda j, i: (i, j))] * len(out_dtypes),
        out_shape=[jax.ShapeDtypeStruct((m, n_cols), dt) for dt in out_dtypes],
        scratch_shapes=[pltpu.VMEM((k, tn), BF16)] * len(ws),
        compiler_params=_params("arbitrary", "arbitrary"),
        name=name,
    )(*x_args, *(w for w, _ in ws))


def _swiglu_combine(g, u):
    return ((g * jax.nn.sigmoid(g)) * u,)


def _identity_combine(y):
    return (y,)


def _conv_gate_combine(gb, gc, hc):
    return gb, gc * hc


def _ffn_up_kernel(n_side, x_ref, wg_ref, wu_ref, wd_ref, *refs):
    if n_side:
        c_ref, wm_ref, b_ref, a_ref, wdb_ref, m_ref, wgb_ref, wub_ref = refs
    else:
        a_ref, wdb_ref, wgb_ref, wub_ref = refs
    _mm_kernel(((1, 0),), 2, 1, _swiglu_combine, x_ref, wg_ref, wu_ref, a_ref, wgb_ref, wub_ref)

    @pl.when(pl.program_id(1) == 0)
    def _():
        wdb_ref[...] = wd_ref[...].astype(BF16)

    if n_side:
        @pl.when(pl.program_id(0) * pl.num_programs(1) + pl.program_id(1) < n_side)
        def _():
            _mod_tile(c_ref, wm_ref, b_ref, m_ref)


def _ffn_up(x, wg, wu, wd, mod=None):
    m, k = x.shape
    f, d = wd.shape
    tn = MXU_COLS
    n_i = m // MM_ROWS
    in_specs = [pl.BlockSpec((MM_ROWS, k), lambda j, i: (i, 0)),
                pl.BlockSpec((k, tn), lambda j, i: (0, j)),
                pl.BlockSpec((k, tn), lambda j, i: (0, j)),
                pl.BlockSpec((tn, d), lambda j, i: (j, 0))]
    out_specs = [pl.BlockSpec((MM_ROWS, tn), lambda j, i: (i, j)),
                 pl.BlockSpec((tn, d), lambda j, i: (j, 0))]
    out_shape = [jax.ShapeDtypeStruct((m, f), BF16), jax.ShapeDtypeStruct((f, d), BF16)]
    args = [x, wg, wu, wd]
    n_side = 0
    if mod is not None:
        cond, w_mod, b_mod2, col0 = mod
        tmod = 128
        n_side = (w_mod.shape[1] - col0) // tmod
        assert n_side <= (f // tn) * n_i
        side = lambda j, i: (0, col0 // tmod + jnp.minimum(j * n_i + i, n_side - 1))
        in_specs += [pl.BlockSpec(cond.shape, lambda j, i: (0, 0)),
                     pl.BlockSpec((w_mod.shape[0], tmod), side),
                     pl.BlockSpec((1, tmod), side)]
        out_specs.append(pl.BlockSpec((cond.shape[0], tmod),
                                      lambda j, i: (0, jnp.minimum(j * n_i + i, n_side - 1))))
        out_shape.append(jax.ShapeDtypeStruct((cond.shape[0], n_side * tmod), F32))
        args += [cond, w_mod, b_mod2]
    return pl.pallas_call(
        functools.partial(_ffn_up_kernel, n_side),
        grid=(f // tn, n_i),
        in_specs=in_specs,
        out_specs=out_specs,
        out_shape=out_shape,
        scratch_shapes=[pltpu.VMEM((k, tn), BF16)] * 2,
        compiler_params=_params("arbitrary", "arbitrary"),
        name="ffn_up_mod" if mod is not None else "ffn_up",
    )(*args)


def _down_kernel(a_ref, w_ref, o_ref):
    o_ref[...] = jnp.dot(a_ref[...], w_ref[...], preferred_element_type=F32)


def _down(a, w):
    m, f = a.shape
    d = w.shape[1]
    tn = MXU_COLS
    return pl.pallas_call(
        _down_kernel,
        grid=(m // MM_ROWS, d // tn),
        in_specs=[pl.BlockSpec((MM_ROWS, f), lambda i, j: (i, 0)),
                  pl.BlockSpec((f, tn), lambda i, j: (0, j))],
        out_specs=pl.BlockSpec((MM_ROWS, tn), lambda i, j: (i, j)),
        out_shape=jax.ShapeDtypeStruct((m, d), F32),
        compiler_params=_params("parallel", "arbitrary"),
        name="ffn_down",
    )(a, w)


def _softmax_with_sink(s, sink):
    m = jnp.maximum(jnp.max(s, axis=-1, keepdims=True), sink)
    p = jnp.exp(s - m)
    denom = jnp.sum(p, axis=-1, keepdims=True) + jnp.exp(sink - m)
    return p * (1.0 / denom)


def _qk(q, k):
    return lax.dot_general(q, k, (((1,), (1,)), ((), ())), preferred_element_type=F32) * ATTN_SCALE


def _group_sinks(sink_ref, h, rows_per_head):
    head = lax.broadcasted_iota(jnp.int32, (GQA_GROUP * rows_per_head, 1), 0) // rows_per_head
    col = jnp.full(head.shape, sink_ref[h * GQA_GROUP], F32)
    for g in range(1, GQA_GROUP):
        col = jnp.where(head == g, sink_ref[h * GQA_GROUP + g], col)
    return col


def _stack_heads(q_ref, prep):
    return jnp.concatenate([prep(q_ref[:, g * HEAD_DIM:(g + 1) * HEAD_DIM]).astype(BF16)
                            for g in range(GQA_GROUP)], axis=0)


def _unstack_heads(o, o_ref):
    rows = o_ref.shape[0]
    for g in range(GQA_GROUP):
        o_ref[:, g * HEAD_DIM:(g + 1) * HEAD_DIM] = o[g * rows:(g + 1) * rows, :]


def _ctx_attn_kernel(sink_ref, q_ref, k_ref, v_ref, g_ref, o_ref, ks_ref, vs_ref, acc_ref):
    ks_ref[...] = k_ref[...]
    vs_ref[...] = v_ref[...]
    qw = GQA_GROUP * HEAD_DIM
    for h in range(N_KV_HEADS):
        kv_cols = slice(h * HEAD_DIM, (h + 1) * HEAD_DIM)
        q = _stack_heads(q_ref.at[:, h * qw:(h + 1) * qw], lambda x: x)
        s = _qk(q, k_ref[:, kv_cols].astype(BF16))
        p = _softmax_with_sink(s, _group_sinks(sink_ref, h, q_ref.shape[0]))
        o = jnp.dot(p.astype(BF16), v_ref[:, kv_cols].astype(BF16), preferred_element_type=F32)
        _unstack_heads(o, acc_ref.at[:, h * qw:(h + 1) * qw])
    o_ref[...] = _rms(acc_ref[...], g_ref[...]).astype(o_ref.dtype)


def _ctx_attention(z, sink, g_a, n_batch, seq):
    q_w = N_Q_HEADS * HEAD_DIM
    kv_w = N_KV_HEADS * HEAD_DIM
    state = jax.ShapeDtypeStruct((n_batch, seq, kv_w), F32)
    return pl.pallas_call(
        _ctx_attn_kernel,
        grid=(n_batch,),
        in_specs=[pl.BlockSpec(memory_space=pltpu.SMEM),
                  pl.BlockSpec((seq, q_w), lambda b: (b, 0)),
                  pl.BlockSpec((seq, kv_w), lambda b: (b, q_w // kv_w)),
                  pl.BlockSpec((seq, kv_w), lambda b: (b, q_w // kv_w + 1)),
                  pl.BlockSpec((1, q_w), lambda b: (0, 0))],
        out_specs=[pl.BlockSpec((seq, q_w), lambda b: (b, 0)),
                   pl.BlockSpec((None, seq, kv_w), lambda b: (b, 0, 0)),
                   pl.BlockSpec((None, seq, kv_w), lambda b: (b, 0, 0))],
        out_shape=[jax.ShapeDtypeStruct((n_batch * seq, q_w), BF16), state, state],
        scratch_shapes=[pltpu.VMEM((seq, q_w), F32)],
        compiler_params=_params("parallel"),
        name="ctx_attention",
    )(sink, z, z, z, g_a.reshape(1, q_w))


def _rope(x, cos, sin_signed):
    lane = lax.broadcasted_iota(jnp.int32, x.shape, 1)
    first = (lane % ROPE_AXIS_DIM) < ROPE_PAIRS
    partner = jnp.where(first, pltpu.roll(x, HEAD_DIM - ROPE_PAIRS, axis=1),
                        pltpu.roll(x, ROPE_PAIRS, axis=1))
    return x * cos + partner * sin_signed


def _lat_attn_kernel(sink_ref, q_ref, k_ref, v_ref, kc_ref, vc_ref, cos_ref, sin_ref, g_ref, o_ref,
                     kr_ref, vb_ref, kcb_ref, vcb_ref, acc_ref):
    i = pl.program_id(1)
    n_blk = pl.num_programs(1)
    blk = q_ref.shape[0]
    qw = GQA_GROUP * HEAD_DIM

    @pl.when(i == 0)
    def _():
        for h in range(N_KV_HEADS):
            kv_cols = slice(h * HEAD_DIM, (h + 1) * HEAD_DIM)
            kr_ref[:, kv_cols] = _rope(k_ref[:, kv_cols], cos_ref[...], sin_ref[...]).astype(BF16)
        vb_ref[...] = v_ref[...].astype(BF16)
        kcb_ref[...] = kc_ref[...].astype(BF16)
        vcb_ref[...] = vc_ref[...].astype(BF16)

    q0 = pl.multiple_of(i * blk, blk)
    w0 = pl.multiple_of(jnp.clip(i - 1, 0, n_blk - 3) * blk, blk)
    cos = cos_ref[pl.ds(q0, blk), :]
    sin = sin_ref[pl.ds(q0, blk), :]
    rows = GQA_GROUP * blk
    q_pos = q0 + lax.broadcasted_iota(jnp.int32, (rows, 3 * blk), 0) % blk
    k_pos = w0 + lax.broadcasted_iota(jnp.int32, (rows, 3 * blk), 1)
    in_window = jnp.abs(k_pos - q_pos) <= WINDOW
    for h in range(N_KV_HEADS):
        kv_cols = slice(h * HEAD_DIM, (h + 1) * HEAD_DIM)
        q = _stack_heads(q_ref.at[:, h * qw:(h + 1) * qw], lambda x: _rope(x, cos, sin))
        s = jnp.concatenate([jnp.where(in_window, _qk(q, kr_ref[pl.ds(w0, 3 * blk), kv_cols]), NEG),
                             _qk(q, kcb_ref[:, kv_cols])], axis=-1)
        p = _softmax_with_sink(s, _group_sinks(sink_ref, h, blk)).astype(BF16)
        o = (jnp.dot(p[:, :3 * blk], vb_ref[pl.ds(w0, 3 * blk), kv_cols], preferred_element_type=F32)
             + jnp.dot(p[:, 3 * blk:], vcb_ref[:, kv_cols], preferred_element_type=F32))
        _unstack_heads(o, acc_ref.at[:, h * qw:(h + 1) * qw])
    o_ref[...] = _rms(acc_ref[...], g_ref[...]).astype(o_ref.dtype)


def _lat_attention(z, cache_k, cache_v, sink, g_a, cos, sin_signed, row0, n_batch, seq):
    blk = WINDOW
    n_blk = seq // blk
    past = cache_k.shape[1]
    q_w = N_Q_HEADS * HEAD_DIM
    kv_w = N_KV_HEADS * HEAD_DIM
    return pl.pallas_call(
        _lat_attn_kernel,
        grid=(n_batch, n_blk),
        in_specs=[pl.BlockSpec(memory_space=pltpu.SMEM),
                  pl.BlockSpec((blk, q_w), lambda b, i: (row0 // blk + b * n_blk + i, 0)),
                  pl.BlockSpec((seq, kv_w), lambda b, i: (row0 // seq + b, q_w // kv_w)),
                  pl.BlockSpec((seq, kv_w), lambda b, i: (row0 // seq + b, q_w // kv_w + 1)),
                  pl.BlockSpec((None, past, kv_w), lambda b, i: (b, 0, 0)),
                  pl.BlockSpec((None, past, kv_w), lambda b, i: (b, 0, 0)),
                  pl.BlockSpec((seq, HEAD_DIM), lambda b, i: (0, 0)),
                  pl.BlockSpec((seq, HEAD_DIM), lambda b, i: (0, 0)),
                  pl.BlockSpec((1, q_w), lambda b, i: (0, 0))],
        out_specs=pl.BlockSpec((blk, q_w), lambda b, i: (b * n_blk + i, 0)),
        out_shape=jax.ShapeDtypeStruct((n_batch * seq, q_w), BF16),
        scratch_shapes=[pltpu.VMEM((seq, kv_w), BF16), pltpu.VMEM((seq, kv_w), BF16),
                        pltpu.VMEM((past, kv_w), BF16), pltpu.VMEM((past, kv_w), BF16),
                        pltpu.VMEM((blk, q_w), F32)],
        compiler_params=_params("parallel", "arbitrary"),
        name="latent_attention",
    )(sink, z, z, z, cache_k, cache_v, cos, sin_signed, g_a.reshape(1, q_w))


def _rope_tables(seq):
    t = np.arange(seq)
    pos = np.stack([t // GRID_W, t % GRID_W], axis=1).astype(np.float64)
    inv_freq = ROPE_THETA ** (-np.arange(ROPE_PAIRS, dtype=np.float64) / ROPE_PAIRS)
    ang = pos[:, :, None] * inv_freq[None, None, :]
    cos, sin = np.cos(ang), np.sin(ang)
    cos = np.concatenate([cos, cos], axis=-1).reshape(seq, HEAD_DIM)
    sin = np.concatenate([-sin, sin], axis=-1).reshape(seq, HEAD_DIM)
    return jnp.asarray(cos, F32), jnp.asarray(sin, F32)


def _conv_norm_kernel(n_ctx, lat_seq, gb_ref, u_ref, up_ref, un_ref, w_ref, gc_ref, o_ref):
    t = pl.program_id(0)
    rows = u_ref.shape[0]
    r0 = t * rows
    is_ctx = r0 < n_ctx
    is_first = jnp.logical_or(is_ctx, (r0 - n_ctx) % lat_seq == 0)
    is_last = jnp.logical_or(is_ctx, (r0 + rows - n_ctx) % lat_seq == 0)
    u = u_ref[...]
    u_before = jnp.where(is_first, 0.0, up_ref[pl.ds(up_ref.shape[0] - 1, 1), :])
    u_after = jnp.where(is_last, 0.0, un_ref[pl.ds(0, 1), :])
    row = lax.broadcasted_iota(jnp.int32, u.shape, 0)
    u_prev = jnp.where(row == 0, u_before, pltpu.roll(u, 1, axis=0))
    u_next = jnp.where(row == rows - 1, u_after, pltpu.roll(u, rows - 1, axis=0))
    y = u_prev * w_ref[pl.ds(0, 1), :] + u * w_ref[pl.ds(1, 1), :] + u_next * w_ref[pl.ds(2, 1), :]
    o_ref[...] = _rms(gb_ref[...] * y, gc_ref[...]).astype(o_ref.dtype)


def _conv_norm(gb, u, w_conv, g_c, n_ctx, ctx_seq, lat_seq):
    assert ctx_seq == ROW_TILE
    n, wc = u.shape
    halo = 8
    per = ROW_TILE // halo
    last_halo = n // halo - 1
    return pl.pallas_call(
        functools.partial(_conv_norm_kernel, n_ctx, lat_seq),
        grid=(n // ROW_TILE,),
        in_specs=[pl.BlockSpec((ROW_TILE, wc), lambda t: (t, 0)),
                  pl.BlockSpec((ROW_TILE, wc), lambda t: (t, 0)),
                  pl.BlockSpec((halo, wc), lambda t: (jnp.maximum(t * per - 1, 0), 0)),
                  pl.BlockSpec((halo, wc), lambda t: (jnp.minimum((t + 1) * per, last_halo), 0)),
                  pl.BlockSpec((w_conv.shape[0], wc), lambda t: (0, 0)),
                  pl.BlockSpec((1, wc), lambda t: (0, 0))],
        out_specs=pl.BlockSpec((ROW_TILE, wc), lambda t: (t, 0)),
        out_shape=jax.ShapeDtypeStruct((n, wc), BF16),
        compiler_params=_params("parallel"),
        name="conv_norm",
    )(gb, u, u, u, w_conv, g_c.reshape(1, wc))


def kernel(x_prompt, x_sample, c, cache_k, cache_v, c_ctx, w_mod, b_mod, g_pre, g_post, w_in, w_conv,
           sink, g_attn_out, g_conv_out, w_o, w_ffn1_gate, w_ffn1_up, w_ffn1_down,
           w_ffn2_gate, w_ffn2_up, w_ffn2_down):
    batch, seq, d = x_prompt.shape
    dec_batch, dec_seq, _ = x_sample.shape
    depth = w_mod.shape[0]
    past = cache_k.shape[2]
    n_ctx = batch * seq
    n_lat = dec_batch * dec_seq
    attn_w = N_Q_HEADS * HEAD_DIM
    kv_w = N_KV_HEADS * HEAD_DIM
    conv_w = d - attn_w
    d_ff = w_ffn1_gate.shape[2]

    hs = (x_prompt.reshape(n_ctx, d), x_sample.reshape(n_lat, d))
    cond = jnp.concatenate([c_ctx[None, :], c, jnp.zeros((8 - 1 - dec_batch, d), F32)], axis=0)
    cos, sin_signed = _rope_tables(dec_seq)
    post = functools.partial(_post, n_ctx=n_ctx, lat_seq=dec_seq)
    wide = 2 * MXU_COLS

    ks_new, vs_new = [], []
    for l in range(depth):
        b_mod2 = b_mod[l].reshape(1, -1)
        mod_head = _modulation(cond, w_mod[l], b_mod2, 2 * d)
        u = _pre(0, hs, mod_head.reshape(8, 2, d), g_pre[l], n_ctx, dec_seq)
        a, wd, mod_tail = _ffn_up(u, w_ffn1_gate[l], w_ffn1_up[l], w_ffn1_down[l],
                                  mod=(cond, w_mod[l], b_mod2, 2 * d))
        mod3 = jnp.concatenate([mod_head, mod_tail], axis=1).reshape(8, 3 * N_SUB, d)
        o = _down(a, wd)
        h, u = post(0, 0.5, hs, o, mod3, g_post[l], g_pre[l], with_pre=True)

        qkv_w = attn_w + 2 * kv_w
        (zqkv,) = _matmul([u], [(w_in[l], 0)], _identity_combine, [F32], qkv_w, wide, "in_proj_qkv")
        gb, uc = _matmul([u], [(w_in[l], qkv_w), (w_in[l], qkv_w + conv_w), (w_in[l], qkv_w + 2 * conv_w)],
                         _conv_gate_combine, [F32, F32], conv_w, MXU_COLS, "in_proj_conv")
        attn_ctx, k_new, v_new = _ctx_attention(zqkv, sink[l], g_attn_out[l], batch, seq)
        attn_lat = _lat_attention(zqkv, cache_k[:, l].reshape(dec_batch, past, kv_w),
                                  cache_v[:, l].reshape(dec_batch, past, kv_w),
                                  sink[l], g_attn_out[l], cos, sin_signed, n_ctx, dec_batch, dec_seq)
        conv_n = _conv_norm(gb, uc, w_conv[l], g_conv_out[l], n_ctx, seq, dec_seq)
        (o,) = _matmul([(attn_ctx, attn_lat), conv_n], [(w_o[l], 0)], _identity_combine, [F32], d, wide,
                       "out_proj")
        h, u = post(1, 1.0, (h,), o, mod3, g_post[l], g_pre[l], with_pre=True)

        a, wd = _ffn_up(u, w_ffn2_gate[l], w_ffn2_up[l], w_ffn2_down[l])
        o = _down(a, wd)
        if l + 1 < depth:
            (h,) = post(2, 0.5, (h,), o, mod3, g_post[l], g_pre[l], with_pre=False)
            hs = (h,)
        ks_new.append(k_new.reshape(batch, seq, N_KV_HEADS, HEAD_DIM))
        vs_new.append(v_new.reshape(batch, seq, N_KV_HEADS, HEAD_DIM))

    (y_prompt,) = post(2, 0.5, (h,), o, mod3, g_post[depth - 1], g_pre[depth - 1], with_pre=False,
                       row0=0, rows=n_ctx)
    (y_sample,) = post(2, 0.5, (h,), o, mod3, g_post[depth - 1], g_pre[depth - 1], with_pre=False,
                       row0=n_ctx, rows=n_lat)
    return (y_prompt.reshape(batch, seq, d), y_sample.reshape(dec_batch, dec_seq, d),
            jnp.stack(ks_new, axis=1), jnp.stack(vs_new, axis=1))
```

```python
import functools

import jax
import jax.numpy as jnp
import numpy as np
from jax import lax
from jax.experimental import pallas as pl
from jax.experimental.pallas import tpu as pltpu

F32 = jnp.float32
BF16 = jnp.bfloat16

N_Q_HEADS = 16
N_KV_HEADS = 4
GQA_GROUP = N_Q_HEADS // N_KV_HEADS
HEAD_DIM = 128
GRID_W = 64
WINDOW = 128
ROPE_THETA = 10000.0
ROPE_AXIS_DIM = HEAD_DIM // 2
ROPE_PAIRS = ROPE_AXIS_DIM // 2
N_SUB = 3
EPS = 1e-6
NEG = -1e30
ATTN_SCALE = HEAD_DIM ** -0.5

VMEM_LIMIT_BYTES = 60 * 1024 * 1024
MXU_COLS = 256
ROW_TILE = 256
MM_ROWS = 1024


def _params(*sem):
    return pltpu.CompilerParams(dimension_semantics=sem, vmem_limit_bytes=VMEM_LIMIT_BYTES)


def _rms(x, g):
    return x * lax.rsqrt(jnp.mean(x * x, axis=-1, keepdims=True) + EPS) * g


def _mod_tile(c_ref, w_ref, b_ref, o_ref):
    x = c_ref[...]
    x = (x * jax.nn.sigmoid(x)).astype(BF16)
    o_ref[...] = jnp.dot(x, w_ref[...].astype(BF16), preferred_element_type=F32) + b_ref[...]


def _modulation(cond, w_mod, b_mod2, n_cols):
    rows, d = cond.shape
    tn = 512
    return pl.pallas_call(
        _mod_tile,
        grid=(n_cols // tn,),
        in_specs=[pl.BlockSpec((rows, d), lambda j: (0, 0)),
                  pl.BlockSpec((d, tn), lambda j: (0, j)),
                  pl.BlockSpec((1, tn), lambda j: (0, j))],
        out_specs=pl.BlockSpec((rows, tn), lambda j: (0, j)),
        out_shape=jax.ShapeDtypeStruct((rows, n_cols), F32),
        compiler_params=_params("arbitrary"),
        name="modulation",
    )(cond, w_mod, b_mod2)


def _load_rows(h_refs, n_ctx_tiles):
    if len(h_refs) == 1:
        return h_refs[0][...]
    return jnp.where(pl.program_id(0) < n_ctx_tiles, h_refs[0][...], h_refs[1][...])


def _pre_kernel(s, n_h, n_ctx_tiles, *refs):
    h_refs, (m_ref, gpre_ref, u_ref) = refs[:n_h], refs[n_h:]
    shift = m_ref[pl.ds(3 * s, 1), :]
    scale = m_ref[pl.ds(3 * s + 1, 1), :]
    u = _rms(_load_rows(h_refs, n_ctx_tiles), gpre_ref[pl.ds(s, 1), :]) * (1.0 + scale) + shift
    u_ref[...] = u.astype(u_ref.dtype)


def _post_kernel(s, res_w, with_pre, n_h, n_ctx_tiles, *refs):
    h_refs, (o_ref, m_ref, gpost_ref, gpre_ref, hn_ref, *u_ref) = refs[:n_h], refs[n_h:]
    gate = m_ref[pl.ds(3 * s + 2, 1), :]
    hn = _load_rows(h_refs, n_ctx_tiles) + (res_w * gate) * _rms(o_ref[...], gpost_ref[pl.ds(s, 1), :])
    hn_ref[...] = hn
    if with_pre:
        shift = m_ref[pl.ds(3 * (s + 1), 1), :]
        scale = m_ref[pl.ds(3 * (s + 1) + 1, 1), :]
        u = _rms(hn, gpre_ref[pl.ds(s + 1, 1), :]) * (1.0 + scale) + shift
        u_ref[0][...] = u.astype(BF16)


def _group_of_tile(i, tile_rows, n_ctx, lat_seq):
    r0 = i * tile_rows
    return jnp.where(r0 < n_ctx, 0, 1 + (r0 - n_ctx) // lat_seq)


def _row_specs(hs, d, t0=0):
    if len(hs) == 1:
        return [pl.BlockSpec((ROW_TILE, d), lambda i: (i + t0, 0))], 0
    t_ctx, t_lat = (x.shape[0] // ROW_TILE for x in hs)
    return [pl.BlockSpec((ROW_TILE, d), lambda i: (jnp.minimum(i, t_ctx - 1), 0)),
            pl.BlockSpec((ROW_TILE, d), lambda i: (jnp.clip(i - t_ctx, 0, t_lat - 1), 0))], t_ctx


def _pre(s, hs, mod3, g_pre, n_ctx, lat_seq):
    n = sum(x.shape[0] for x in hs)
    d = hs[0].shape[1]
    grp = functools.partial(_group_of_tile, tile_rows=ROW_TILE, n_ctx=n_ctx, lat_seq=lat_seq)
    h_specs, t_ctx = _row_specs(hs, d)
    return pl.pallas_call(
        functools.partial(_pre_kernel, s, len(hs), t_ctx),
        grid=(n // ROW_TILE,),
        in_specs=h_specs + [pl.BlockSpec((None, mod3.shape[1], d), lambda i: (grp(i), 0, 0)),
                            pl.BlockSpec((N_SUB, d), lambda i: (0, 0))],
        out_specs=pl.BlockSpec((ROW_TILE, d), lambda i: (i, 0)),
        out_shape=jax.ShapeDtypeStruct((n, d), BF16),
        compiler_params=_params("parallel"),
        name=f"pre{s}",
    )(*hs, mod3, g_pre)


def _post(s, res_w, hs, o, mod3, g_post, g_pre, n_ctx, lat_seq, with_pre, row0=0, rows=None):
    d = o.shape[1]
    rows = o.shape[0] if rows is None else rows
    t0 = row0 // ROW_TILE
    grp = lambda i: _group_of_tile(i + t0, ROW_TILE, n_ctx, lat_seq)
    h_specs, t_ctx = _row_specs(hs, d, t0)
    out_shape = [jax.ShapeDtypeStruct((rows, d), F32)]
    out_specs = [pl.BlockSpec((ROW_TILE, d), lambda i: (i, 0))]
    if with_pre:
        out_shape.append(jax.ShapeDtypeStruct((rows, d), BF16))
        out_specs.append(pl.BlockSpec((ROW_TILE, d), lambda i: (i, 0)))
    return pl.pallas_call(
        functools.partial(_post_kernel, s, res_w, with_pre, len(hs), t_ctx),
        grid=(rows // ROW_TILE,),
        in_specs=h_specs + [pl.BlockSpec((ROW_TILE, d), lambda i: (i + t0, 0)),
                            pl.BlockSpec((None, 3 * N_SUB, d), lambda i: (grp(i), 0, 0)),
                            pl.BlockSpec((N_SUB, d), lambda i: (0, 0)),
                            pl.BlockSpec((N_SUB, d), lambda i: (0, 0))],
        out_specs=out_specs,
        out_shape=out_shape,
        compiler_params=_params("parallel"),
        name=f"post{s}",
    )(*hs, o, mod3, g_post, g_pre)


def _mm_kernel(slabs, n_w, n_out, combine, *refs):
    n_x = sum(n for n, _ in slabs)
    x_refs, refs = refs[:n_x], refs[n_x:]
    w_refs, o_refs, wb_refs = refs[:n_w], refs[n_w:n_w + n_out], refs[n_w + n_out:]

    @pl.when(pl.program_id(1) == 0)
    def _():
        for w_ref, wb_ref in zip(w_refs, wb_refs):
            wb_ref[...] = w_ref[...].astype(BF16)

    xs, r = [], 0
    for n, t_first in slabs:
        if n == 1:
            xs.append(x_refs[r][...])
        else:
            xs.append(jnp.where(pl.program_id(1) < t_first, x_refs[r][...], x_refs[r + 1][...]))
        r += n
    ys = []
    for wb_ref in wb_refs:
        k0, y = 0, None
        for x in xs:
            k1 = k0 + x.shape[1]
            part = jnp.dot(x, wb_ref[k0:k1, :], preferred_element_type=F32)
            y = part if y is None else y + part
            k0 = k1
        ys.append(y)
    for o_ref, y in zip(o_refs, combine(*ys)):
        o_ref[...] = y.astype(o_ref.dtype)


def _matmul(xs, ws, combine, out_dtypes, n_cols, tn, name):
    x_specs, x_args, slabs = [], [], []
    for x in xs:
        if isinstance(x, tuple):
            t_a, t_b = (p.shape[0] // MM_ROWS for p in x)
            x_specs += [pl.BlockSpec((MM_ROWS, x[0].shape[1]), lambda j, i, t=t_a: (jnp.minimum(i, t - 1), 0)),
                        pl.BlockSpec((MM_ROWS, x[1].shape[1]),
                                     lambda j, i, t=t_a, u=t_b: (jnp.clip(i - t, 0, u - 1), 0))]
            x_args += list(x)
            slabs.append((2, t_a))
        else:
            x_specs.append(pl.BlockSpec((MM_ROWS, x.shape[1]), lambda j, i: (i, 0)))
            x_args.append(x)
            slabs.append((1, 0))
    m = sum(p.shape[0] for p in xs[0]) if isinstance(xs[0], tuple) else xs[0].shape[0]
    k = sum((x[0] if isinstance(x, tuple) else x).shape[1] for x in xs)
    w_specs = [pl.BlockSpec((k, tn), lambda j, i, c0=col0 // tn: (0, j + c0)) for _, col0 in ws]
    return pl.pallas_call(
        functools.partial(_mm_kernel, tuple(slabs), len(ws), len(out_dtypes), combine),
        grid=(n_cols // tn, m // MM_ROWS),
        in_specs=x_specs + w_specs,
        out_specs=[pl.BlockSpec((MM_ROWS, tn), lambda j, i: (i, j))] * len(out_dtypes),
        out_shape=[jax.ShapeDtypeStruct((m, n_cols), dt) for dt in out_dtypes],
        scratch_shapes=[pltpu.VMEM((k, tn), BF16)] * len(ws),
        compiler_params=_params("arbitrary", "arbitrary"),
        name=name,
    )(*x_args, *(w for w, _ in ws))


def _swiglu_combine(g, u):
    return ((g * jax.nn.sigmoid(g)) * u,)


def _identity_combine(y):
    return (y,)


def _conv_gate_combine(gb, gc, hc):
    return gb, gc * hc


def _ffn_up_kernel(n_side, x_ref, wg_ref, wu_ref, wd_ref, *refs):
    if n_side:
        c_ref, wm_ref, b_ref, a_ref, wdb_ref, m_ref, wgb_ref, wub_ref = refs
    else:
        a_ref, wdb_ref, wgb_ref, wub_ref = refs
    _mm_kernel(((1, 0),), 2, 1, _swiglu_combine, x_ref, wg_ref, wu_ref, a_ref, wgb_ref, wub_ref)

    wdb_ref[...] = wd_ref[...].astype(BF16)

    if n_side:
        @pl.when(pl.program_id(0) * pl.num_programs(1) + pl.program_id(1) < n_side)
        def _():
            _mod_tile(c_ref, wm_ref, b_ref, m_ref)


def _ffn_up(x, wg, wu, wd, mod=None):
    m, k = x.shape
    f, d = wd.shape
    tn = MXU_COLS
    n_i = m // MM_ROWS
    wd_rows = tn // n_i
    assert wd_rows * n_i == tn and wd_rows % 16 == 0
    in_specs = [pl.BlockSpec((MM_ROWS, k), lambda j, i: (i, 0)),
                pl.BlockSpec((k, tn), lambda j, i: (0, j)),
                pl.BlockSpec((k, tn), lambda j, i: (0, j)),
                pl.BlockSpec((wd_rows, d), lambda j, i: (j * n_i + i, 0))]
    out_specs = [pl.BlockSpec((MM_ROWS, tn), lambda j, i: (i, j)),
                 pl.BlockSpec((wd_rows, d), lambda j, i: (j * n_i + i, 0))]
    out_shape = [jax.ShapeDtypeStruct((m, f), BF16), jax.ShapeDtypeStruct((f, d), BF16)]
    args = [x, wg, wu, wd]
    n_side = 0
    if mod is not None:
        cond, w_mod, b_mod2, col0 = mod
        tmod = 128
        n_side = (w_mod.shape[1] - col0) // tmod
        assert n_side <= (f // tn) * n_i
        side = lambda j, i: (0, col0 // tmod + jnp.minimum(j * n_i + i, n_side - 1))
        in_specs += [pl.BlockSpec(cond.shape, lambda j, i: (0, 0)),
                     pl.BlockSpec((w_mod.shape[0], tmod), side),
                     pl.BlockSpec((1, tmod), side)]
        out_specs.append(pl.BlockSpec((cond.shape[0], tmod),
                                      lambda j, i: (0, jnp.minimum(j * n_i + i, n_side - 1))))
        out_shape.append(jax.ShapeDtypeStruct((cond.shape[0], n_side * tmod), F32))
        args += [cond, w_mod, b_mod2]
    return pl.pallas_call(
        functools.partial(_ffn_up_kernel, n_side),
        grid=(f // tn, n_i),
        in_specs=in_specs,
        out_specs=out_specs,
        out_shape=out_shape,
        scratch_shapes=[pltpu.VMEM((k, tn), BF16)] * 2,
        compiler_params=_params("arbitrary", "arbitrary"),
        name="ffn_up_mod" if mod is not None else "ffn_up",
    )(*args)


def _down_kernel(a_ref, w_ref, o_ref):
    o_ref[...] = jnp.dot(a_ref[...], w_ref[...], preferred_element_type=F32)


def _down(a, w):
    m, f = a.shape
    d = w.shape[1]
    tn = MXU_COLS
    return pl.pallas_call(
        _down_kernel,
        grid=(m // MM_ROWS, d // tn),
        in_specs=[pl.BlockSpec((MM_ROWS, f), lambda i, j: (i, 0)),
                  pl.BlockSpec((f, tn), lambda i, j: (0, j))],
        out_specs=pl.BlockSpec((MM_ROWS, tn), lambda i, j: (i, j)),
        out_shape=jax.ShapeDtypeStruct((m, d), F32),
        compiler_params=_params("parallel", "arbitrary"),
        name="ffn_down",
    )(a, w)


def _softmax_with_sink(s, sink):
    m = jnp.maximum(jnp.max(s, axis=-1, keepdims=True), sink)
    p = jnp.exp(s - m)
    denom = jnp.sum(p, axis=-1, keepdims=True) + jnp.exp(sink - m)
    return p * (1.0 / denom)


def _qk(q, k):
    return lax.dot_general(q, k, (((1,), (1,)), ((), ())), preferred_element_type=F32) * ATTN_SCALE


def _group_sinks(sink_ref, h, rows_per_head):
    head = lax.broadcasted_iota(jnp.int32, (GQA_GROUP * rows_per_head, 1), 0) // rows_per_head
    col = jnp.full(head.shape, sink_ref[h * GQA_GROUP], F32)
    for g in range(1, GQA_GROUP):
        col = jnp.where(head == g, sink_ref[h * GQA_GROUP + g], col)
    return col


def _stack_heads(q_ref, prep):
    return jnp.concatenate([prep(q_ref[:, g * HEAD_DIM:(g + 1) * HEAD_DIM]).astype(BF16)
                            for g in range(GQA_GROUP)], axis=0)


def _unstack_heads(o, o_ref):
    rows = o_ref.shape[0]
    for g in range(GQA_GROUP):
        o_ref[:, g * HEAD_DIM:(g + 1) * HEAD_DIM] = o[g * rows:(g + 1) * rows, :]


def _ctx_attn_kernel(sink_ref, q_ref, k_ref, v_ref, g_ref, o_ref, ks_ref, vs_ref, acc_ref):
    ks_ref[...] = k_ref[...]
    vs_ref[...] = v_ref[...]
    qw = GQA_GROUP * HEAD_DIM
    for h in range(N_KV_HEADS):
        kv_cols = slice(h * HEAD_DIM, (h + 1) * HEAD_DIM)
        q = _stack_heads(q_ref.at[:, h * qw:(h + 1) * qw], lambda x: x)
        s = _qk(q, k_ref[:, kv_cols].astype(BF16))
        p = _softmax_with_sink(s, _group_sinks(sink_ref, h, q_ref.shape[0]))
        o = jnp.dot(p.astype(BF16), v_ref[:, kv_cols].astype(BF16), preferred_element_type=F32)
        _unstack_heads(o, acc_ref.at[:, h * qw:(h + 1) * qw])
    o_ref[...] = _rms(acc_ref[...], g_ref[...]).astype(o_ref.dtype)


def _ctx_attention(z, sink, g_a, n_batch, seq):
    q_w = N_Q_HEADS * HEAD_DIM
    kv_w = N_KV_HEADS * HEAD_DIM
    state = jax.ShapeDtypeStruct((n_batch, seq, kv_w), F32)
    return pl.pallas_call(
        _ctx_attn_kernel,
        grid=(n_batch,),
        in_specs=[pl.BlockSpec(memory_space=pltpu.SMEM),
                  pl.BlockSpec((seq, q_w), lambda b: (b, 0)),
                  pl.BlockSpec((seq, kv_w), lambda b: (b, q_w // kv_w)),
                  pl.BlockSpec((seq, kv_w), lambda b: (b, q_w // kv_w + 1)),
                  pl.BlockSpec((1, q_w), lambda b: (0, 0))],
        out_specs=[pl.BlockSpec((seq, q_w), lambda b: (b, 0)),
                   pl.BlockSpec((None, seq, kv_w), lambda b: (b, 0, 0)),
                   pl.BlockSpec((None, seq, kv_w), lambda b: (b, 0, 0))],
        out_shape=[jax.ShapeDtypeStruct((n_batch * seq, q_w), BF16), state, state],
        scratch_shapes=[pltpu.VMEM((seq, q_w), F32)],
        compiler_params=_params("parallel"),
        name="ctx_attention",
    )(sink, z, z, z, g_a.reshape(1, q_w))


def _rope(x, cos, sin_signed):
    lane = lax.broadcasted_iota(jnp.int32, x.shape, 1)
    first = (lane % ROPE_AXIS_DIM) < ROPE_PAIRS
    partner = jnp.where(first, pltpu.roll(x, HEAD_DIM - ROPE_PAIRS, axis=1),
                        pltpu.roll(x, ROPE_PAIRS, axis=1))
    return x * cos + partner * sin_signed


def _lat_attn_kernel(sink_ref, q_ref, k_ref, v_ref, kc_ref, vc_ref, cos_ref, sin_ref, g_ref, o_ref,
                     kr_ref, vb_ref, kcb_ref, vcb_ref, acc_ref):
    i = pl.program_id(1)
    n_blk = pl.num_programs(1)
    blk = q_ref.shape[0]
    qw = GQA_GROUP * HEAD_DIM

    @pl.when(i == 0)
    def _():
        for h in range(N_KV_HEADS):
            kv_cols = slice(h * HEAD_DIM, (h + 1) * HEAD_DIM)
            kr_ref[:, kv_cols] = _rope(k_ref[:, kv_cols], cos_ref[...], sin_ref[...]).astype(BF16)
        vb_ref[...] = v_ref[...].astype(BF16)
        kcb_ref[...] = kc_ref[...].astype(BF16)
        vcb_ref[...] = vc_ref[...].astype(BF16)

    q0 = pl.multiple_of(i * blk, blk)
    w0 = pl.multiple_of(jnp.clip(i - 1, 0, n_blk - 3) * blk, blk)
    cos = cos_ref[pl.ds(q0, blk), :]
    sin = sin_ref[pl.ds(q0, blk), :]
    rows = GQA_GROUP * blk
    q_pos = q0 + lax.broadcasted_iota(jnp.int32, (rows, 3 * blk), 0) % blk
    k_pos = w0 + lax.broadcasted_iota(jnp.int32, (rows, 3 * blk), 1)
    in_window = jnp.abs(k_pos - q_pos) <= WINDOW
    for h in range(N_KV_HEADS):
        kv_cols = slice(h * HEAD_DIM, (h + 1) * HEAD_DIM)
        q = _stack_heads(q_ref.at[:, h * qw:(h + 1) * qw], lambda x: _rope(x, cos, sin))
        s = jnp.concatenate([jnp.where(in_window, _qk(q, kr_ref[pl.ds(w0, 3 * blk), kv_cols]), NEG),
                             _qk(q, kcb_ref[:, kv_cols])], axis=-1)
        p = _softmax_with_sink(s, _group_sinks(sink_ref, h, blk)).astype(BF16)
        o = (jnp.dot(p[:, :3 * blk], vb_ref[pl.ds(w0, 3 * blk), kv_cols], preferred_element_type=F32)
             + jnp.dot(p[:, 3 * blk:], vcb_ref[:, kv_cols], preferred_element_type=F32))
        _unstack_heads(o, acc_ref.at[:, h * qw:(h + 1) * qw])
    o_ref[...] = _rms(acc_ref[...], g_ref[...]).astype(o_ref.dtype)


def _lat_attention(z, cache_k, cache_v, sink, g_a, cos, sin_signed, row0, n_batch, seq):
    blk = WINDOW
    n_blk = seq // blk
    past = cache_k.shape[1]
    q_w = N_Q_HEADS * HEAD_DIM
    kv_w = N_KV_HEADS * HEAD_DIM
    return pl.pallas_call(
        _lat_attn_kernel,
        grid=(n_batch, n_blk),
        in_specs=[pl.BlockSpec(memory_space=pltpu.SMEM),
                  pl.BlockSpec((blk, q_w), lambda b, i: (row0 // blk + b * n_blk + i, 0)),
                  pl.BlockSpec((seq, kv_w), lambda b, i: (row0 // seq + b, q_w // kv_w)),
                  pl.BlockSpec((seq, kv_w), lambda b, i: (row0 // seq + b, q_w // kv_w + 1)),
                  pl.BlockSpec((None, past, kv_w), lambda b, i: (b, 0, 0)),
                  pl.BlockSpec((None, past, kv_w), lambda b, i: (b, 0, 0)),
                  pl.BlockSpec((seq, HEAD_DIM), lambda b, i: (0, 0)),
                  pl.BlockSpec((seq, HEAD_DIM), lambda b, i: (0, 0)),
                  pl.BlockSpec((1, q_w), lambda b, i: (0, 0))],
        out_specs=pl.BlockSpec((blk, q_w), lambda b, i: (b * n_blk + i, 0)),
        out_shape=jax.ShapeDtypeStruct((n_batch * seq, q_w), BF16),
        scratch_shapes=[pltpu.VMEM((seq, kv_w), BF16), pltpu.VMEM((seq, kv_w), BF16),
                        pltpu.VMEM((past, kv_w), BF16), pltpu.VMEM((past, kv_w), BF16),
                        pltpu.VMEM((blk, q_w), F32)],
        compiler_params=_params("parallel", "arbitrary"),
        name="latent_attention",
    )(sink, z, z, z, cache_k, cache_v, cos, sin_signed, g_a.reshape(1, q_w))


def _rope_tables(seq):
    t = np.arange(seq)
    pos = np.stack([t // GRID_W, t % GRID_W], axis=1).astype(np.float64)
    inv_freq = ROPE_THETA ** (-np.arange(ROPE_PAIRS, dtype=np.float64) / ROPE_PAIRS)
    ang = pos[:, :, None] * inv_freq[None, None, :]
    cos, sin = np.cos(ang), np.sin(ang)
    cos = np.concatenate([cos, cos], axis=-1).reshape(seq, HEAD_DIM)
    sin = np.concatenate([-sin, sin], axis=-1).reshape(seq, HEAD_DIM)
    return jnp.asarray(cos, F32), jnp.asarray(sin, F32)


def _conv_norm_kernel(n_ctx, lat_seq, gb_ref, u_ref, up_ref, un_ref, w_ref, gc_ref, o_ref):
    t = pl.program_id(0)
    rows = u_ref.shape[0]
    r0 = t * rows
    is_ctx = r0 < n_ctx
    is_first = jnp.logical_or(is_ctx, (r0 - n_ctx) % lat_seq == 0)
    is_last = jnp.logical_or(is_ctx, (r0 + rows - n_ctx) % lat_seq == 0)
    u = u_ref[...]
    u_before = jnp.where(is_first, 0.0, up_ref[pl.ds(up_ref.shape[0] - 1, 1), :])
    u_after = jnp.where(is_last, 0.0, un_ref[pl.ds(0, 1), :])
    row = lax.broadcasted_iota(jnp.int32, u.shape, 0)
    u_prev = jnp.where(row == 0, u_before, pltpu.roll(u, 1, axis=0))
    u_next = jnp.where(row == rows - 1, u_after, pltpu.roll(u, rows - 1, axis=0))
    y = u_prev * w_ref[pl.ds(0, 1), :] + u * w_ref[pl.ds(1, 1), :] + u_next * w_ref[pl.ds(2, 1), :]
    o_ref[...] = _rms(gb_ref[...] * y, gc_ref[...]).astype(o_ref.dtype)


def _conv_norm(gb, u, w_conv, g_c, n_ctx, ctx_seq, lat_seq):
    assert ctx_seq == ROW_TILE
    n, wc = u.shape
    halo = 8
    per = ROW_TILE // halo
    last_halo = n // halo - 1
    return pl.pallas_call(
        functools.partial(_conv_norm_kernel, n_ctx, lat_seq),
        grid=(n // ROW_TILE,),
        in_specs=[pl.BlockSpec((ROW_TILE, wc), lambda t: (t, 0)),
                  pl.BlockSpec((ROW_TILE, wc), lambda t: (t, 0)),
                  pl.BlockSpec((halo, wc), lambda t: (jnp.maximum(t * per - 1, 0), 0)),
                  pl.BlockSpec((halo, wc), lambda t: (jnp.minimum((t + 1) * per, last_halo), 0)),
                  pl.BlockSpec((w_conv.shape[0], wc), lambda t: (0, 0)),
                  pl.BlockSpec((1, wc), lambda t: (0, 0))],
        out_specs=pl.BlockSpec((ROW_TILE, wc), lambda t: (t, 0)),
        out_shape=jax.ShapeDtypeStruct((n, wc), BF16),
        compiler_params=_params("parallel"),
        name="conv_norm",
    )(gb, u, u, u, w_conv, g_c.reshape(1, wc))


def kernel(x_prompt, x_sample, c, cache_k, cache_v, c_ctx, w_mod, b_mod, g_pre, g_post, w_in, w_conv,
           sink, g_attn_out, g_conv_out, w_o, w_ffn1_gate, w_ffn1_up, w_ffn1_down,
           w_ffn2_gate, w_ffn2_up, w_ffn2_down):
    batch, seq, d = x_prompt.shape
    dec_batch, dec_seq, _ = x_sample.shape
    depth = w_mod.shape[0]
    past = cache_k.shape[2]
    n_ctx = batch * seq
    n_lat = dec_batch * dec_seq
    attn_w = N_Q_HEADS * HEAD_DIM
    kv_w = N_KV_HEADS * HEAD_DIM
    conv_w = d - attn_w
    d_ff = w_ffn1_gate.shape[2]

    hs = (x_prompt.reshape(n_ctx, d), x_sample.reshape(n_lat, d))
    cond = jnp.concatenate([c_ctx[None, :], c, jnp.zeros((8 - 1 - dec_batch, d), F32)], axis=0)
    cos, sin_signed = _rope_tables(dec_seq)
    post = functools.partial(_post, n_ctx=n_ctx, lat_seq=dec_seq)
    wide = 2 * MXU_COLS

    ks_new, vs_new = [], []
    for l in range(depth):
        b_mod2 = b_mod[l].reshape(1, -1)
        mod_head = _modulation(cond, w_mod[l], b_mod2, 2 * d)
        u = _pre(0, hs, mod_head.reshape(8, 2, d), g_pre[l], n_ctx, dec_seq)
        a, wd, mod_tail = _ffn_up(u, w_ffn1_gate[l], w_ffn1_up[l], w_ffn1_down[l],
                                  mod=(cond, w_mod[l], b_mod2, 2 * d))
        mod3 = jnp.concatenate([mod_head, mod_tail], axis=1).reshape(8, 3 * N_SUB, d)
        o = _down(a, wd)
        h, u = post(0, 0.5, hs, o, mod3, g_post[l], g_pre[l], with_pre=True)

        qkv_w = attn_w + 2 * kv_w
        (zqkv,) = _matmul([u], [(w_in[l], 0)], _identity_combine, [F32], qkv_w, wide, "in_proj_qkv")
        gb, uc = _matmul([u], [(w_in[l], qkv_w), (w_in[l], qkv_w + conv_w), (w_in[l], qkv_w + 2 * conv_w)],
                         _conv_gate_combine, [F32, F32], conv_w, MXU_COLS, "in_proj_conv")
        attn_ctx, k_new, v_new = _ctx_attention(zqkv, sink[l], g_attn_out[l], batch, seq)
        attn_lat = _lat_attention(zqkv, cache_k[:, l].reshape(dec_batch, past, kv_w),
                                  cache_v[:, l].reshape(dec_batch, past, kv_w),
                                  sink[l], g_attn_out[l], cos, sin_signed, n_ctx, dec_batch, dec_seq)
        conv_n = _conv_norm(gb, uc, w_conv[l], g_conv_out[l], n_ctx, seq, dec_seq)
        (o,) = _matmul([(attn_ctx, attn_lat), conv_n], [(w_o[l], 0)], _identity_combine, [F32], d, wide,
                       "out_proj")
        h, u = post(1, 1.0, (h,), o, mod3, g_post[l], g_pre[l], with_pre=True)

        a, wd = _ffn_up(u, w_ffn2_gate[l], w_ffn2_up[l], w_ffn2_down[l])
        o = _down(a, wd)
        if l + 1 < depth:
            (h,) = post(2, 0.5, (h,), o, mod3, g_post[l], g_pre[l], with_pre=False)
            hs = (h,)
        ks_new.append(k_new.reshape(batch, seq, N_KV_HEADS, HEAD_DIM))
        vs_new.append(v_new.reshape(batch, seq, N_KV_HEADS, HEAD_DIM))

    (y_prompt,) = post(2, 0.5, (h,), o, mod3, g_post[depth - 1], g_pre[depth - 1], with_pre=False,
                       row0=0, rows=n_ctx)
    (y_sample,) = post(2, 0.5, (h,), o, mod3, g_post[depth - 1], g_pre[depth - 1], with_pre=False,
                       row0=n_ctx, rows=n_lat)
    return (y_prompt.reshape(batch, seq, d), y_sample.reshape(dec_batch, dec_seq, d),
            jnp.stack(ks_new, axis=1), jnp.stack(vs_new, axis=1))
```

```python
import functools

import jax
import jax.numpy as jnp
import numpy as np
from jax import lax
from jax.experimental import pallas as pl
from jax.experimental.pallas import tpu as pltpu

F32 = jnp.float32
BF16 = jnp.bfloat16

N_Q_HEADS = 16
N_KV_HEADS = 4
GQA_GROUP = N_Q_HEADS // N_KV_HEADS
HEAD_DIM = 128
GRID_W = 64
WINDOW = 128
ROPE_THETA = 10000.0
ROPE_AXIS_DIM = HEAD_DIM // 2
ROPE_PAIRS = ROPE_AXIS_DIM // 2
N_SUB = 3
EPS = 1e-6
NEG = -1e30
ATTN_SCALE = HEAD_DIM ** -0.5

VMEM_LIMIT_BYTES = 60 * 1024 * 1024
MXU_COLS = 256
ROW_TILE = 256
MM_ROWS = 1024


def _params(*sem):
    return pltpu.CompilerParams(dimension_semantics=sem, vmem_limit_bytes=VMEM_LIMIT_BYTES)


def _rms(x, g):
    return x * lax.rsqrt(jnp.mean(x * x, axis=-1, keepdims=True) + EPS) * g


def _mod_tile(c_ref, w_ref, b_ref, o_ref):
    x = c_ref[...]
    x = (x * jax.nn.sigmoid(x)).astype(BF16)
    o_ref[...] = jnp.dot(x, w_ref[...].astype(BF16), preferred_element_type=F32) + b_ref[...]


def _modulation(cond, w_mod, b_mod2, n_cols):
    rows, d = cond.shape
    tn = 512
    return pl.pallas_call(
        _mod_tile,
        grid=(n_cols // tn,),
        in_specs=[pl.BlockSpec((rows, d), lambda j: (0, 0)),
                  pl.BlockSpec((d, tn), lambda j: (0, j)),
                  pl.BlockSpec((1, tn), lambda j: (0, j))],
        out_specs=pl.BlockSpec((rows, tn), lambda j: (0, j)),
        out_shape=jax.ShapeDtypeStruct((rows, n_cols), F32),
        compiler_params=_params("arbitrary"),
        name="modulation",
    )(cond, w_mod, b_mod2)


def _load_rows(h_refs, n_ctx_tiles):
    if len(h_refs) == 1:
        return h_refs[0][...]
    return jnp.where(pl.program_id(0) < n_ctx_tiles, h_refs[0][...], h_refs[1][...])


def _pre_kernel(s, n_h, n_ctx_tiles, *refs):
    h_refs, (m_ref, gpre_ref, u_ref) = refs[:n_h], refs[n_h:]
    shift = m_ref[pl.ds(3 * s, 1), :]
    scale = m_ref[pl.ds(3 * s + 1, 1), :]
    u = _rms(_load_rows(h_refs, n_ctx_tiles), gpre_ref[pl.ds(s, 1), :]) * (1.0 + scale) + shift
    u_ref[...] = u.astype(u_ref.dtype)


def _post_kernel(s, res_w, with_pre, n_h, n_ctx_tiles, *refs):
    h_refs, (o_ref, m_ref, gpost_ref, gpre_ref, hn_ref, *u_ref) = refs[:n_h], refs[n_h:]
    gate = m_ref[pl.ds(3 * s + 2, 1), :]
    hn = _load_rows(h_refs, n_ctx_tiles) + (res_w * gate) * _rms(o_ref[...], gpost_ref[pl.ds(s, 1), :])
    hn_ref[...] = hn
    if with_pre:
        shift = m_ref[pl.ds(3 * (s + 1), 1), :]
        scale = m_ref[pl.ds(3 * (s + 1) + 1, 1), :]
        u = _rms(hn, gpre_ref[pl.ds(s + 1, 1), :]) * (1.0 + scale) + shift
        u_ref[0][...] = u.astype(BF16)


def _group_of_tile(i, tile_rows, n_ctx, lat_seq):
    r0 = i * tile_rows
    return jnp.where(r0 < n_ctx, 0, 1 + (r0 - n_ctx) // lat_seq)


def _row_specs(hs, d, t0=0):
    if len(hs) == 1:
        return [pl.BlockSpec((ROW_TILE, d), lambda i: (i + t0, 0))], 0
    t_ctx, t_lat = (x.shape[0] // ROW_TILE for x in hs)
    return [pl.BlockSpec((ROW_TILE, d), lambda i: (jnp.minimum(i, t_ctx - 1), 0)),
            pl.BlockSpec((ROW_TILE, d), lambda i: (jnp.clip(i - t_ctx, 0, t_lat - 1), 0))], t_ctx


def _pre(s, hs, mod3, g_pre, n_ctx, lat_seq):
    n = sum(x.shape[0] for x in hs)
    d = hs[0].shape[1]
    grp = functools.partial(_group_of_tile, tile_rows=ROW_TILE, n_ctx=n_ctx, lat_seq=lat_seq)
    h_specs, t_ctx = _row_specs(hs, d)
    return pl.pallas_call(
        functools.partial(_pre_kernel, s, len(hs), t_ctx),
        grid=(n // ROW_TILE,),
        in_specs=h_specs + [pl.BlockSpec((None, mod3.shape[1], d), lambda i: (grp(i), 0, 0)),
                            pl.BlockSpec((N_SUB, d), lambda i: (0, 0))],
        out_specs=pl.BlockSpec((ROW_TILE, d), lambda i: (i, 0)),
        out_shape=jax.ShapeDtypeStruct((n, d), BF16),
        compiler_params=_params("parallel"),
        name=f"pre{s}",
    )(*hs, mod3, g_pre)


def _post(s, res_w, hs, o, mod3, g_post, g_pre, n_ctx, lat_seq, with_pre, row0=0, rows=None):
    d = o.shape[1]
    rows = o.shape[0] if rows is None else rows
    t0 = row0 // ROW_TILE
    grp = lambda i: _group_of_tile(i + t0, ROW_TILE, n_ctx, lat_seq)
    h_specs, t_ctx = _row_specs(hs, d, t0)
    out_shape = [jax.ShapeDtypeStruct((rows, d), F32)]
    out_specs = [pl.BlockSpec((ROW_TILE, d), lambda i: (i, 0))]
    if with_pre:
        out_shape.append(jax.ShapeDtypeStruct((rows, d), BF16))
        out_specs.append(pl.BlockSpec((ROW_TILE, d), lambda i: (i, 0)))
    return pl.pallas_call(
        functools.partial(_post_kernel, s, res_w, with_pre, len(hs), t_ctx),
        grid=(rows // ROW_TILE,),
        in_specs=h_specs + [pl.BlockSpec((ROW_TILE, d), lambda i: (i + t0, 0)),
                            pl.BlockSpec((None, 3 * N_SUB, d), lambda i: (grp(i), 0, 0)),
                            pl.BlockSpec((N_SUB, d), lambda i: (0, 0)),
                            pl.BlockSpec((N_SUB, d), lambda i: (0, 0))],
        out_specs=out_specs,
        out_shape=out_shape,
        compiler_params=_params("parallel"),
        name=f"post{s}",
    )(*hs, o, mod3, g_post, g_pre)


def _mm_kernel(slabs, n_w, n_out, combine, *refs):
    n_x = sum(n for n, _ in slabs)
    x_refs, refs = refs[:n_x], refs[n_x:]
    w_refs, o_refs, wb_refs = refs[:n_w], refs[n_w:n_w + n_out], refs[n_w + n_out:]

    @pl.when(pl.program_id(1) == 0)
    def _():
        for w_ref, wb_ref in zip(w_refs, wb_refs):
            wb_ref[...] = w_ref[...].astype(BF16)

    xs, r = [], 0
    for n, t_first in slabs:
        if n == 1:
            xs.append(x_refs[r][...])
        else:
            xs.append(jnp.where(pl.program_id(1) < t_first, x_refs[r][...], x_refs[r + 1][...]))
        r += n
    ys = []
    for wb_ref in wb_refs:
        k0, y = 0, None
        for x in xs:
            k1 = k0 + x.shape[1]
            part = jnp.dot(x, wb_ref[k0:k1, :], preferred_element_type=F32)
            y = part if y is None else y + part
            k0 = k1
        ys.append(y)
    for o_ref, y in zip(o_refs, combine(*ys)):
        o_ref[...] = y.astype(o_ref.dtype)


def _matmul(xs, ws, combine, out_dtypes, n_cols, tn, name):
    x_specs, x_args, slabs = [], [], []
    for x in xs:
        if isinstance(x, tuple):
            t_a, t_b = (p.shape[0] // MM_ROWS for p in x)
            x_specs += [pl.BlockSpec((MM_ROWS, x[0].shape[1]), lambda j, i, t=t_a: (jnp.minimum(i, t - 1), 0)),
                        pl.BlockSpec((MM_ROWS, x[1].shape[1]),
                                     lambda j, i, t=t_a, u=t_b: (jnp.clip(i - t, 0, u - 1), 0))]
            x_args += list(x)
            slabs.append((2, t_a))
        else:
            x_specs.append(pl.BlockSpec((MM_ROWS, x.shape[1]), lambda j, i: (i, 0)))
            x_args.append(x)
            slabs.append((1, 0))
    m = sum(p.shape[0] for p in xs[0]) if isinstance(xs[0], tuple) else xs[0].shape[0]
    k = sum((x[0] if isinstance(x, tuple) else x).shape[1] for x in xs)
    w_specs = [pl.BlockSpec((k, tn), lambda j, i, c0=col0 // tn: (0, j + c0)) for _, col0 in ws]
    return pl.pallas_call(
        functools.partial(_mm_kernel, tuple(slabs), len(ws), len(out_dtypes), combine),
        grid=(n_cols // tn, m // MM_ROWS),
        in_specs=x_specs + w_specs,
        out_specs=[pl.BlockSpec((MM_ROWS, tn), lambda j, i: (i, j))] * len(out_dtypes),
        out_shape=[jax.ShapeDtypeStruct((m, n_cols), dt) for dt in out_dtypes],
        scratch_shapes=[pltpu.VMEM((k, tn), BF16)] * len(ws),
        compiler_params=_params("arbitrary", "arbitrary"),
        name=name,
    )(*x_args, *(w for w, _ in ws))


def _swiglu_combine(g, u):
    return ((g * jax.nn.sigmoid(g)) * u,)


def _identity_combine(y):
    return (y,)


def _conv_gate_combine(gb, gc, hc):
    return gb, gc * hc


def _wd_block(step, period, shift, n_blocks):
    return jnp.minimum((step + shift) // period, n_blocks - 1)


def _ffn_up_kernel(n_side, wd_sched, x_ref, wg_ref, wu_ref, wd_ref, *refs):
    if n_side:
        c_ref, wm_ref, b_ref, a_ref, wdb_ref, m_ref, wgb_ref, wub_ref = refs
    else:
        a_ref, wdb_ref, wgb_ref, wub_ref = refs
    _mm_kernel(((1, 0),), 2, 1, _swiglu_combine, x_ref, wg_ref, wu_ref, a_ref, wgb_ref, wub_ref)
    step = pl.program_id(0) * pl.num_programs(1) + pl.program_id(1)

    period, shift, _ = wd_sched
    @pl.when(jnp.logical_or(step == 0, (step + shift) % period == 0))
    def _():
        wdb_ref[...] = wd_ref[...].astype(BF16)

    if n_side:
        @pl.when(step < n_side)
        def _():
            _mod_tile(c_ref, wm_ref, b_ref, m_ref)


def _ffn_up(x, wg, wu, wd, mod=None):
    m, k = x.shape
    f, d = wd.shape
    tn = MXU_COLS
    n_i = m // MM_ROWS
    assert n_i % 4 == 0 and tn % 2 == 0
    wd_sched = (n_i // 2, n_i // 4, 2 * (f // tn))
    wd_map = lambda j, i: (_wd_block(j * n_i + i, *wd_sched), 0)
    in_specs = [pl.BlockSpec((MM_ROWS, k), lambda j, i: (i, 0)),
                pl.BlockSpec((k, tn), lambda j, i: (0, j)),
                pl.BlockSpec((k, tn), lambda j, i: (0, j)),
                pl.BlockSpec((tn // 2, d), wd_map)]
    out_specs = [pl.BlockSpec((MM_ROWS, tn), lambda j, i: (i, j)),
                 pl.BlockSpec((tn // 2, d), wd_map)]
    out_shape = [jax.ShapeDtypeStruct((m, f), BF16), jax.ShapeDtypeStruct((f, d), BF16)]
    args = [x, wg, wu, wd]
    n_side = 0
    if mod is not None:
        cond, w_mod, b_mod2, col0 = mod
        tmod = 128
        n_side = (w_mod.shape[1] - col0) // tmod
        assert n_side <= (f // tn) * n_i
        side = lambda j, i: (0, col0 // tmod + jnp.minimum(j * n_i + i, n_side - 1))
        in_specs += [pl.BlockSpec(cond.shape, lambda j, i: (0, 0)),
                     pl.BlockSpec((w_mod.shape[0], tmod), side),
                     pl.BlockSpec((1, tmod), side)]
        out_specs.append(pl.BlockSpec((cond.shape[0], tmod),
                                      lambda j, i: (0, jnp.minimum(j * n_i + i, n_side - 1))))
        out_shape.append(jax.ShapeDtypeStruct((cond.shape[0], n_side * tmod), F32))
        args += [cond, w_mod, b_mod2]
    return pl.pallas_call(
        functools.partial(_ffn_up_kernel, n_side, wd_sched),
        grid=(f // tn, n_i),
        in_specs=in_specs,
        out_specs=out_specs,
        out_shape=out_shape,
        scratch_shapes=[pltpu.VMEM((k, tn), BF16)] * 2,
        compiler_params=_params("arbitrary", "arbitrary"),
        name="ffn_up_mod" if mod is not None else "ffn_up",
    )(*args)


def _down_kernel(a_ref, w_ref, o_ref):
    o_ref[...] = jnp.dot(a_ref[...], w_ref[...], preferred_element_type=F32)


def _down(a, w):
    m, f = a.shape
    d = w.shape[1]
    tn = MXU_COLS
    return pl.pallas_call(
        _down_kernel,
        grid=(m // MM_ROWS, d // tn),
        in_specs=[pl.BlockSpec((MM_ROWS, f), lambda i, j: (i, 0)),
                  pl.BlockSpec((f, tn), lambda i, j: (0, j))],
        out_specs=pl.BlockSpec((MM_ROWS, tn), lambda i, j: (i, j)),
        out_shape=jax.ShapeDtypeStruct((m, d), F32),
        compiler_params=_params("parallel", "arbitrary"),
        name="ffn_down",
    )(a, w)


def _softmax_with_sink(s, sink):
    m = jnp.maximum(jnp.max(s, axis=-1, keepdims=True), sink)
    p = jnp.exp(s - m)
    denom = jnp.sum(p, axis=-1, keepdims=True) + jnp.exp(sink - m)
    return p * (1.0 / denom)


def _qk(q, k):
    return lax.dot_general(q, k, (((1,), (1,)), ((), ())), preferred_element_type=F32) * ATTN_SCALE


def _group_sinks(sink_ref, h, rows_per_head):
    head = lax.broadcasted_iota(jnp.int32, (GQA_GROUP * rows_per_head, 1), 0) // rows_per_head
    col = jnp.full(head.shape, sink_ref[h * GQA_GROUP], F32)
    for g in range(1, GQA_GROUP):
        col = jnp.where(head == g, sink_ref[h * GQA_GROUP + g], col)
    return col


def _stack_heads(q_ref, prep):
    return jnp.concatenate([prep(q_ref[:, g * HEAD_DIM:(g + 1) * HEAD_DIM]).astype(BF16)
                            for g in range(GQA_GROUP)], axis=0)


def _unstack_heads(o, o_ref):
    rows = o_ref.shape[0]
    for g in range(GQA_GROUP):
        o_ref[:, g * HEAD_DIM:(g + 1) * HEAD_DIM] = o[g * rows:(g + 1) * rows, :]


def _ctx_attn_kernel(sink_ref, q_ref, k_ref, v_ref, g_ref, o_ref, ks_ref, vs_ref, acc_ref):
    ks_ref[...] = k_ref[...]
    vs_ref[...] = v_ref[...]
    qw = GQA_GROUP * HEAD_DIM
    for h in range(N_KV_HEADS):
        kv_cols = slice(h * HEAD_DIM, (h + 1) * HEAD_DIM)
        q = _stack_heads(q_ref.at[:, h * qw:(h + 1) * qw], lambda x: x)
        s = _qk(q, k_ref[:, kv_cols].astype(BF16))
        p = _softmax_with_sink(s, _group_sinks(sink_ref, h, q_ref.shape[0]))
        o = jnp.dot(p.astype(BF16), v_ref[:, kv_cols].astype(BF16), preferred_element_type=F32)
        _unstack_heads(o, acc_ref.at[:, h * qw:(h + 1) * qw])
    o_ref[...] = _rms(acc_ref[...], g_ref[...]).astype(o_ref.dtype)


def _ctx_attention(z, sink, g_a, n_batch, seq):
    q_w = N_Q_HEADS * HEAD_DIM
    kv_w = N_KV_HEADS * HEAD_DIM
    state = jax.ShapeDtypeStruct((n_batch, seq, kv_w), F32)
    return pl.pallas_call(
        _ctx_attn_kernel,
        grid=(n_batch,),
        in_specs=[pl.BlockSpec(memory_space=pltpu.SMEM),
                  pl.BlockSpec((seq, q_w), lambda b: (b, 0)),
                  pl.BlockSpec((seq, kv_w), lambda b: (b, q_w // kv_w)),
                  pl.BlockSpec((seq, kv_w), lambda b: (b, q_w // kv_w + 1)),
                  pl.BlockSpec((1, q_w), lambda b: (0, 0))],
        out_specs=[pl.BlockSpec((seq, q_w), lambda b: (b, 0)),
                   pl.BlockSpec((None, seq, kv_w), lambda b: (b, 0, 0)),
                   pl.BlockSpec((None, seq, kv_w), lambda b: (b, 0, 0))],
        out_shape=[jax.ShapeDtypeStruct((n_batch * seq, q_w), BF16), state, state],
        scratch_shapes=[pltpu.VMEM((seq, q_w), F32)],
        compiler_params=_params("parallel"),
        name="ctx_attention",
    )(sink, z, z, z, g_a.reshape(1, q_w))


def _rope(x, cos, sin_signed):
    lane = lax.broadcasted_iota(jnp.int32, x.shape, 1)
    first = (lane % ROPE_AXIS_DIM) < ROPE_PAIRS
    partner = jnp.where(first, pltpu.roll(x, HEAD_DIM - ROPE_PAIRS, axis=1),
                        pltpu.roll(x, ROPE_PAIRS, axis=1))
    return x * cos + partner * sin_signed


def _lat_attn_kernel(sink_ref, q_ref, k_ref, v_ref, kc_ref, vc_ref, cos_ref, sin_ref, g_ref, o_ref,
                     kr_ref, vb_ref, kcb_ref, vcb_ref, acc_ref):
    i = pl.program_id(1)
    n_blk = pl.num_programs(1)
    blk = q_ref.shape[0]
    qw = GQA_GROUP * HEAD_DIM

    @pl.when(i == 0)
    def _():
        for h in range(N_KV_HEADS):
            kv_cols = slice(h * HEAD_DIM, (h + 1) * HEAD_DIM)
            kr_ref[:, kv_cols] = _rope(k_ref[:, kv_cols], cos_ref[...], sin_ref[...]).astype(BF16)
        vb_ref[...] = v_ref[...].astype(BF16)
        kcb_ref[...] = kc_ref[...].astype(BF16)
        vcb_ref[...] = vc_ref[...].astype(BF16)

    q0 = pl.multiple_of(i * blk, blk)
    w0 = pl.multiple_of(jnp.clip(i - 1, 0, n_blk - 3) * blk, blk)
    cos = cos_ref[pl.ds(q0, blk), :]
    sin = sin_ref[pl.ds(q0, blk), :]
    rows = GQA_GROUP * blk
    q_pos = q0 + lax.broadcasted_iota(jnp.int32, (rows, 3 * blk), 0) % blk
    k_pos = w0 + lax.broadcasted_iota(jnp.int32, (rows, 3 * blk), 1)
    in_window = jnp.abs(k_pos - q_pos) <= WINDOW
    for h in range(N_KV_HEADS):
        kv_cols = slice(h * HEAD_DIM, (h + 1) * HEAD_DIM)
        q = _stack_heads(q_ref.at[:, h * qw:(h + 1) * qw], lambda x: _rope(x, cos, sin))
        s = jnp.concatenate([jnp.where(in_window, _qk(q, kr_ref[pl.ds(w0, 3 * blk), kv_cols]), NEG),
                             _qk(q, kcb_ref[:, kv_cols])], axis=-1)
        p = _softmax_with_sink(s, _group_sinks(sink_ref, h, blk)).astype(BF16)
        o = (jnp.dot(p[:, :3 * blk], vb_ref[pl.ds(w0, 3 * blk), kv_cols], preferred_element_type=F32)
             + jnp.dot(p[:, 3 * blk:], vcb_ref[:, kv_cols], preferred_element_type=F32))
        _unstack_heads(o, acc_ref.at[:, h * qw:(h + 1) * qw])
    o_ref[...] = _rms(acc_ref[...], g_ref[...]).astype(o_ref.dtype)


def _lat_attention(z, cache_k, cache_v, sink, g_a, cos, sin_signed, row0, n_batch, seq):
    blk = WINDOW
    n_blk = seq // blk
    past = cache_k.shape[1]
    q_w = N_Q_HEADS * HEAD_DIM
    kv_w = N_KV_HEADS * HEAD_DIM
    return pl.pallas_call(
        _lat_attn_kernel,
        grid=(n_batch, n_blk),
        in_specs=[pl.BlockSpec(memory_space=pltpu.SMEM),
                  pl.BlockSpec((blk, q_w), lambda b, i: (row0 // blk + b * n_blk + i, 0)),
                  pl.BlockSpec((seq, kv_w), lambda b, i: (row0 // seq + b, q_w // kv_w)),
                  pl.BlockSpec((seq, kv_w), lambda b, i: (row0 // seq + b, q_w // kv_w + 1)),
                  pl.BlockSpec((None, past, kv_w), lambda b, i: (b, 0, 0)),
                  pl.BlockSpec((None, past, kv_w), lambda b, i: (b, 0, 0)),
                  pl.BlockSpec((seq, HEAD_DIM), lambda b, i: (0, 0)),
                  pl.BlockSpec((seq, HEAD_DIM), lambda b, i: (0, 0)),
                  pl.BlockSpec((1, q_w), lambda b, i: (0, 0))],
        out_specs=pl.BlockSpec((blk, q_w), lambda b, i: (b * n_blk + i, 0)),
        out_shape=jax.ShapeDtypeStruct((n_batch * seq, q_w), BF16),
        scratch_shapes=[pltpu.VMEM((seq, kv_w), BF16), pltpu.VMEM((seq, kv_w), BF16),
                        pltpu.VMEM((past, kv_w), BF16), pltpu.VMEM((past, kv_w), BF16),
                        pltpu.VMEM((blk, q_w), F32)],
        compiler_params=_params("parallel", "arbitrary"),
        name="latent_attention",
    )(sink, z, z, z, cache_k, cache_v, cos, sin_signed, g_a.reshape(1, q_w))


def _rope_tables(seq):
    t = np.arange(seq)
    pos = np.stack([t // GRID_W, t % GRID_W], axis=1).astype(np.float64)
    inv_freq = ROPE_THETA ** (-np.arange(ROPE_PAIRS, dtype=np.float64) / ROPE_PAIRS)
    ang = pos[:, :, None] * inv_freq[None, None, :]
    cos, sin = np.cos(ang), np.sin(ang)
    cos = np.concatenate([cos, cos], axis=-1).reshape(seq, HEAD_DIM)
    sin = np.concatenate([-sin, sin], axis=-1).reshape(seq, HEAD_DIM)
    return jnp.asarray(cos, F32), jnp.asarray(sin, F32)


def _conv_norm_kernel(n_ctx, lat_seq, gb_ref, u_ref, up_ref, un_ref, w_ref, gc_ref, o_ref):
    t = pl.program_id(0)
    rows = u_ref.shape[0]
    r0 = t * rows
    is_ctx = r0 < n_ctx
    is_first = jnp.logical_or(is_ctx, (r0 - n_ctx) % lat_seq == 0)
    is_last = jnp.logical_or(is_ctx, (r0 + rows - n_ctx) % lat_seq == 0)
    u = u_ref[...]
    u_before = jnp.where(is_first, 0.0, up_ref[pl.ds(up_ref.shape[0] - 1, 1), :])
    u_after = jnp.where(is_last, 0.0, un_ref[pl.ds(0, 1), :])
    row = lax.broadcasted_iota(jnp.int32, u.shape, 0)
    u_prev = jnp.where(row == 0, u_before, pltpu.roll(u, 1, axis=0))
    u_next = jnp.where(row == rows - 1, u_after, pltpu.roll(u, rows - 1, axis=0))
    y = u_prev * w_ref[pl.ds(0, 1), :] + u * w_ref[pl.ds(1, 1), :] + u_next * w_ref[pl.ds(2, 1), :]
    o_ref[...] = _rms(gb_ref[...] * y, gc_ref[...]).astype(o_ref.dtype)


def _conv_norm(gb, u, w_conv, g_c, n_ctx, ctx_seq, lat_seq):
    assert ctx_seq == ROW_TILE
    n, wc = u.shape
    halo = 8
    per = ROW_TILE // halo
    last_halo = n // halo - 1
    return pl.pallas_call(
        functools.partial(_conv_norm_kernel, n_ctx, lat_seq),
        grid=(n // ROW_TILE,),
        in_specs=[pl.BlockSpec((ROW_TILE, wc), lambda t: (t, 0)),
                  pl.BlockSpec((ROW_TILE, wc), lambda t: (t, 0)),
                  pl.BlockSpec((halo, wc), lambda t: (jnp.maximum(t * per - 1, 0), 0)),
                  pl.BlockSpec((halo, wc), lambda t: (jnp.minimum((t + 1) * per, last_halo), 0)),
                  pl.BlockSpec((w_conv.shape[0], wc), lambda t: (0, 0)),
                  pl.BlockSpec((1, wc), lambda t: (0, 0))],
        out_specs=pl.BlockSpec((ROW_TILE, wc), lambda t: (t, 0)),
        out_shape=jax.ShapeDtypeStruct((n, wc), BF16),
        compiler_params=_params("parallel"),
        name="conv_norm",
    )(gb, u, u, u, w_conv, g_c.reshape(1, wc))


def kernel(x_prompt, x_sample, c, cache_k, cache_v, c_ctx, w_mod, b_mod, g_pre, g_post, w_in, w_conv,
           sink, g_attn_out, g_conv_out, w_o, w_ffn1_gate, w_ffn1_up, w_ffn1_down,
           w_ffn2_gate, w_ffn2_up, w_ffn2_down):
    batch, seq, d = x_prompt.shape
    dec_batch, dec_seq, _ = x_sample.shape
    depth = w_mod.shape[0]
    past = cache_k.shape[2]
    n_ctx = batch * seq
    n_lat = dec_batch * dec_seq
    attn_w = N_Q_HEADS * HEAD_DIM
    kv_w = N_KV_HEADS * HEAD_DIM
    conv_w = d - attn_w
    d_ff = w_ffn1_gate.shape[2]

    hs = (x_prompt.reshape(n_ctx, d), x_sample.reshape(n_lat, d))
    cond = jnp.concatenate([c_ctx[None, :], c, jnp.zeros((8 - 1 - dec_batch, d), F32)], axis=0)
    cos, sin_signed = _rope_tables(dec_seq)
    post = functools.partial(_post, n_ctx=n_ctx, lat_seq=dec_seq)
    wide = 2 * MXU_COLS

    ks_new, vs_new = [], []
    for l in range(depth):
        b_mod2 = b_mod[l].reshape(1, -1)
        mod_head = _modulation(cond, w_mod[l], b_mod2, 2 * d)
        u = _pre(0, hs, mod_head.reshape(8, 2, d), g_pre[l], n_ctx, dec_seq)
        a, wd, mod_tail = _ffn_up(u, w_ffn1_gate[l], w_ffn1_up[l], w_ffn1_down[l],
                                  mod=(cond, w_mod[l], b_mod2, 2 * d))
        mod3 = jnp.concatenate([mod_head, mod_tail], axis=1).reshape(8, 3 * N_SUB, d)
        o = _down(a, wd)
        h, u = post(0, 0.5, hs, o, mod3, g_post[l], g_pre[l], with_pre=True)

        qkv_w = attn_w + 2 * kv_w
        (zqkv,) = _matmul([u], [(w_in[l], 0)], _identity_combine, [F32], qkv_w, wide, "in_proj_qkv")
        gb, uc = _matmul([u], [(w_in[l], qkv_w), (w_in[l], qkv_w + conv_w), (w_in[l], qkv_w + 2 * conv_w)],
                         _conv_gate_combine, [F32, F32], conv_w, MXU_COLS, "in_proj_conv")
        attn_ctx, k_new, v_new = _ctx_attention(zqkv, sink[l], g_attn_out[l], batch, seq)
        attn_lat = _lat_attention(zqkv, cache_k[:, l].reshape(dec_batch, past, kv_w),
                                  cache_v[:, l].reshape(dec_batch, past, kv_w),
                                  sink[l], g_attn_out[l], cos, sin_signed, n_ctx, dec_batch, dec_seq)
        conv_n = _conv_norm(gb, uc, w_conv[l], g_conv_out[l], n_ctx, seq, dec_seq)
        (o,) = _matmul([(attn_ctx, attn_lat), conv_n], [(w_o[l], 0)], _identity_combine, [F32], d, wide,
                       "out_proj")
        h, u = post(1, 1.0, (h,), o, mod3, g_post[l], g_pre[l], with_pre=True)

        a, wd = _ffn_up(u, w_ffn2_gate[l], w_ffn2_up[l], w_ffn2_down[l])
        o = _down(a, wd)
        if l + 1 < depth:
            (h,) = post(2, 0.5, (h,), o, mod3, g_post[l], g_pre[l], with_pre=False)
            hs = (h,)
        ks_new.append(k_new.reshape(batch, seq, N_KV_HEADS, HEAD_DIM))
        vs_new.append(v_new.reshape(batch, seq, N_KV_HEADS, HEAD_DIM))

    (y_prompt,) = post(2, 0.5, (h,), o, mod3, g_post[depth - 1], g_pre[depth - 1], with_pre=False,
                       row0=0, rows=n_ctx)
    (y_sample,) = post(2, 0.5, (h,), o, mod3, g_post[depth - 1], g_pre[depth - 1], with_pre=False,
                       row0=n_ctx, rows=n_lat)
    return (y_prompt.reshape(batch, seq, d), y_sample.reshape(dec_batch, dec_seq, d),
            jnp.stack(ks_new, axis=1), jnp.stack(vs_new, axis=1))
```

```python
import functools

import jax
import jax.numpy as jnp
import numpy as np
from jax import lax
from jax.experimental import pallas as pl
from jax.experimental.pallas import tpu as pltpu

F32 = jnp.float32
BF16 = jnp.bfloat16

N_Q_HEADS = 16
N_KV_HEADS = 4
GQA_GROUP = N_Q_HEADS // N_KV_HEADS
HEAD_DIM = 128
GRID_W = 64
WINDOW = 128
ROPE_THETA = 10000.0
ROPE_AXIS_DIM = HEAD_DIM // 2
ROPE_PAIRS = ROPE_AXIS_DIM // 2
N_SUB = 3
EPS = 1e-6
NEG = -1e30
ATTN_SCALE = HEAD_DIM ** -0.5

VMEM_LIMIT_BYTES = 60 * 1024 * 1024
MXU_COLS = 256
ROW_TILE = 256
MM_ROWS = 1024


def _params(*sem):
    return pltpu.CompilerParams(dimension_semantics=sem, vmem_limit_bytes=VMEM_LIMIT_BYTES)


def _rms(x, g):
    return x * lax.rsqrt(jnp.mean(x * x, axis=-1, keepdims=True) + EPS) * g


def _mod_tile(c_ref, w_ref, b_ref, o_ref):
    x = c_ref[...]
    x = (x * jax.nn.sigmoid(x)).astype(BF16)
    o_ref[...] = jnp.dot(x, w_ref[...].astype(BF16), preferred_element_type=F32) + b_ref[...]


def _modulation(cond, w_mod, b_mod2, n_cols):
    rows, d = cond.shape
    tn = 512
    return pl.pallas_call(
        _mod_tile,
        grid=(n_cols // tn,),
        in_specs=[pl.BlockSpec((rows, d), lambda j: (0, 0)),
                  pl.BlockSpec((d, tn), lambda j: (0, j)),
                  pl.BlockSpec((1, tn), lambda j: (0, j))],
        out_specs=pl.BlockSpec((rows, tn), lambda j: (0, j)),
        out_shape=jax.ShapeDtypeStruct((rows, n_cols), F32),
        compiler_params=_params("arbitrary"),
        name="modulation",
    )(cond, w_mod, b_mod2)


def _load_rows(h_refs, n_ctx_tiles):
    if len(h_refs) == 1:
        return h_refs[0][...]
    return jnp.where(pl.program_id(0) < n_ctx_tiles, h_refs[0][...], h_refs[1][...])


def _pre_kernel(s, n_h, n_ctx_tiles, *refs):
    h_refs, (m_ref, gpre_ref, u_ref) = refs[:n_h], refs[n_h:]
    shift = m_ref[pl.ds(3 * s, 1), :]
    scale = m_ref[pl.ds(3 * s + 1, 1), :]
    u = _rms(_load_rows(h_refs, n_ctx_tiles), gpre_ref[pl.ds(s, 1), :]) * (1.0 + scale) + shift
    u_ref[...] = u.astype(u_ref.dtype)


def _post_kernel(s, res_w, with_pre, n_h, n_ctx_tiles, *refs):
    h_refs, (o_ref, m_ref, gpost_ref, gpre_ref, hn_ref, *u_ref) = refs[:n_h], refs[n_h:]
    gate = m_ref[pl.ds(3 * s + 2, 1), :]
    hn = _load_rows(h_refs, n_ctx_tiles) + (res_w * gate) * _rms(o_ref[...], gpost_ref[pl.ds(s, 1), :])
    hn_ref[...] = hn
    if with_pre:
        shift = m_ref[pl.ds(3 * (s + 1), 1), :]
        scale = m_ref[pl.ds(3 * (s + 1) + 1, 1), :]
        u = _rms(hn, gpre_ref[pl.ds(s + 1, 1), :]) * (1.0 + scale) + shift
        u_ref[0][...] = u.astype(BF16)


def _group_of_tile(i, tile_rows, n_ctx, lat_seq):
    r0 = i * tile_rows
    return jnp.where(r0 < n_ctx, 0, 1 + (r0 - n_ctx) // lat_seq)


def _row_specs(hs, d, t0=0):
    if len(hs) == 1:
        return [pl.BlockSpec((ROW_TILE, d), lambda i: (i + t0, 0))], 0
    t_ctx, t_lat = (x.shape[0] // ROW_TILE for x in hs)
    return [pl.BlockSpec((ROW_TILE, d), lambda i: (jnp.minimum(i, t_ctx - 1), 0)),
            pl.BlockSpec((ROW_TILE, d), lambda i: (jnp.clip(i - t_ctx, 0, t_lat - 1), 0))], t_ctx


def _pre(s, hs, mod3, g_pre, n_ctx, lat_seq):
    n = sum(x.shape[0] for x in hs)
    d = hs[0].shape[1]
    grp = functools.partial(_group_of_tile, tile_rows=ROW_TILE, n_ctx=n_ctx, lat_seq=lat_seq)
    h_specs, t_ctx = _row_specs(hs, d)
    return pl.pallas_call(
        functools.partial(_pre_kernel, s, len(hs), t_ctx),
        grid=(n // ROW_TILE,),
        in_specs=h_specs + [pl.BlockSpec((None, mod3.shape[1], d), lambda i: (grp(i), 0, 0)),
                            pl.BlockSpec((N_SUB, d), lambda i: (0, 0))],
        out_specs=pl.BlockSpec((ROW_TILE, d), lambda i: (i, 0)),
        out_shape=jax.ShapeDtypeStruct((n, d), BF16),
        compiler_params=_params("parallel"),
        name=f"pre{s}",
    )(*hs, mod3, g_pre)


def _post(s, res_w, hs, o, mod3, g_post, g_pre, n_ctx, lat_seq, with_pre, row0=0, rows=None):
    d = o.shape[1]
    rows = o.shape[0] if rows is None else rows
    t0 = row0 // ROW_TILE
    grp = lambda i: _group_of_tile(i + t0, ROW_TILE, n_ctx, lat_seq)
    h_specs, t_ctx = _row_specs(hs, d, t0)
    out_shape = [jax.ShapeDtypeStruct((rows, d), F32)]
    out_specs = [pl.BlockSpec((ROW_TILE, d), lambda i: (i, 0))]
    if with_pre:
        out_shape.append(jax.ShapeDtypeStruct((rows, d), BF16))
        out_specs.append(pl.BlockSpec((ROW_TILE, d), lambda i: (i, 0)))
    return pl.pallas_call(
        functools.partial(_post_kernel, s, res_w, with_pre, len(hs), t_ctx),
        grid=(rows // ROW_TILE,),
        in_specs=h_specs + [pl.BlockSpec((ROW_TILE, d), lambda i: (i + t0, 0)),
                            pl.BlockSpec((None, 3 * N_SUB, d), lambda i: (grp(i), 0, 0)),
                            pl.BlockSpec((N_SUB, d), lambda i: (0, 0)),
                            pl.BlockSpec((N_SUB, d), lambda i: (0, 0))],
        out_specs=out_specs,
        out_shape=out_shape,
        compiler_params=_params("parallel"),
        name=f"post{s}",
    )(*hs, o, mod3, g_post, g_pre)


def _mm_kernel(slabs, n_w, n_out, combine, *refs):
    n_x = sum(n for n, _ in slabs)
    x_refs, refs = refs[:n_x], refs[n_x:]
    w_refs, o_refs, wb_refs = refs[:n_w], refs[n_w:n_w + n_out], refs[n_w + n_out:]

    @pl.when(pl.program_id(1) == 0)
    def _():
        for w_ref, wb_ref in zip(w_refs, wb_refs):
            wb_ref[...] = w_ref[...].astype(BF16)

    xs, r = [], 0
    for n, t_first in slabs:
        if n == 1:
            xs.append(x_refs[r][...])
        else:
            xs.append(jnp.where(pl.program_id(1) < t_first, x_refs[r][...], x_refs[r + 1][...]))
        r += n
    ys = []
    for wb_ref in wb_refs:
        k0, y = 0, None
        for x in xs:
            k1 = k0 + x.shape[1]
            part = jnp.dot(x, wb_ref[k0:k1, :], preferred_element_type=F32)
            y = part if y is None else y + part
            k0 = k1
        ys.append(y)
    for o_ref, y in zip(o_refs, combine(*ys)):
        o_ref[...] = y.astype(o_ref.dtype)


def _matmul(xs, ws, combine, out_dtypes, n_cols, tn, name):
    x_specs, x_args, slabs = [], [], []
    for x in xs:
        if isinstance(x, tuple):
            t_a, t_b = (p.shape[0] // MM_ROWS for p in x)
            x_specs += [pl.BlockSpec((MM_ROWS, x[0].shape[1]), lambda j, i, t=t_a: (jnp.minimum(i, t - 1), 0)),
                        pl.BlockSpec((MM_ROWS, x[1].shape[1]),
                                     lambda j, i, t=t_a, u=t_b: (jnp.clip(i - t, 0, u - 1), 0))]
            x_args += list(x)
            slabs.append((2, t_a))
        else:
            x_specs.append(pl.BlockSpec((MM_ROWS, x.shape[1]), lambda j, i: (i, 0)))
            x_args.append(x)
            slabs.append((1, 0))
    m = sum(p.shape[0] for p in xs[0]) if isinstance(xs[0], tuple) else xs[0].shape[0]
    k = sum((x[0] if isinstance(x, tuple) else x).shape[1] for x in xs)
    w_specs = [pl.BlockSpec((k, tn), lambda j, i, c0=col0 // tn: (0, j + c0)) for _, col0 in ws]
    return pl.pallas_call(
        functools.partial(_mm_kernel, tuple(slabs), len(ws), len(out_dtypes), combine),
        grid=(n_cols // tn, m // MM_ROWS),
        in_specs=x_specs + w_specs,
        out_specs=[pl.BlockSpec((MM_ROWS, tn), lambda j, i: (i, j))] * len(out_dtypes),
        out_shape=[jax.ShapeDtypeStruct((m, n_cols), dt) for dt in out_dtypes],
        scratch_shapes=[pltpu.VMEM((k, tn), BF16)] * len(ws),
        compiler_params=_params("arbitrary", "arbitrary"),
        name=name,
    )(*x_args, *(w for w, _ in ws))


X_RING = 3


def _mm_ring_kernel(n_w, n_out, combine, x_hbm, *refs):
    w_refs, o_refs = refs[:n_w], refs[n_w:n_w + n_out]
    xbuf, sem = refs[n_w + n_out:n_w + n_out + 2]
    wb_refs = refs[n_w + n_out + 2:]
    n_i = pl.num_programs(1)
    step = pl.program_id(0) * n_i + pl.program_id(1)
    n_steps = pl.num_programs(0) * n_i
    tm = xbuf.shape[1]

    def x_copy(s):
        row = pl.multiple_of((s % n_i) * tm, tm)
        slot = s % X_RING
        return pltpu.make_async_copy(x_hbm.at[pl.ds(row, tm), :], xbuf.at[slot], sem.at[slot])

    @pl.when(step == 0)
    def _():
        for s in range(X_RING - 1):
            x_copy(s).start()

    @pl.when(step + X_RING - 1 < n_steps)
    def _():
        x_copy(step + X_RING - 1).start()

    @pl.when(pl.program_id(1) == 0)
    def _():
        for w_ref, wb_ref in zip(w_refs, wb_refs):
            wb_ref[...] = w_ref[...].astype(BF16)

    x_copy(step).wait()
    x = xbuf[step % X_RING]
    ys = [jnp.dot(x, wb_ref[...], preferred_element_type=F32) for wb_ref in wb_refs]
    for o_ref, y in zip(o_refs, combine(*ys)):
        o_ref[...] = y.astype(o_ref.dtype)


def _matmul_ring(x, ws, combine, out_dtypes, n_cols, tn, name):
    m, k = x.shape
    assert (n_cols // tn) * (m // MM_ROWS) >= X_RING - 1
    w_specs = [pl.BlockSpec((k, tn), lambda j, i, c0=col0 // tn: (0, j + c0)) for _, col0 in ws]
    return pl.pallas_call(
        functools.partial(_mm_ring_kernel, len(ws), len(out_dtypes), combine),
        grid=(n_cols // tn, m // MM_ROWS),
        in_specs=[pl.BlockSpec(memory_space=pl.ANY)] + w_specs,
        out_specs=[pl.BlockSpec((MM_ROWS, tn), lambda j, i: (i, j))] * len(out_dtypes),
        out_shape=[jax.ShapeDtypeStruct((m, n_cols), dt) for dt in out_dtypes],
        scratch_shapes=[pltpu.VMEM((X_RING, MM_ROWS, k), BF16), pltpu.SemaphoreType.DMA((X_RING,))]
                       + [pltpu.VMEM((k, tn), BF16)] * len(ws),
        compiler_params=_params("arbitrary", "arbitrary"),
        name=name,
    )(x, *(w for w, _ in ws))


def _swiglu_combine(g, u):
    return ((g * jax.nn.sigmoid(g)) * u,)


def _identity_combine(y):
    return (y,)


def _conv_gate_combine(gb, gc, hc):
    return gb, gc * hc


def _ffn_up_kernel(n_side, x_ref, wg_ref, wu_ref, wd_ref, *refs):
    if n_side:
        c_ref, wm_ref, b_ref, a_ref, wdb_ref, m_ref, xbuf, sem, wgb_ref, wub_ref = refs
    else:
        a_ref, wdb_ref, xbuf, sem, wgb_ref, wub_ref = refs
    _mm_ring_kernel(2, 1, _swiglu_combine, x_ref, wg_ref, wu_ref, a_ref, xbuf, sem, wgb_ref, wub_ref)

    wdb_ref[...] = wd_ref[...].astype(BF16)

    if n_side:
        @pl.when(pl.program_id(0) * pl.num_programs(1) + pl.program_id(1) < n_side)
        def _():
            _mod_tile(c_ref, wm_ref, b_ref, m_ref)


def _ffn_up(x, wg, wu, wd, mod=None):
    m, k = x.shape
    f, d = wd.shape
    tn = MXU_COLS
    n_i = m // MM_ROWS
    wd_rows = tn // n_i
    assert wd_rows * n_i == tn and wd_rows % 16 == 0
    in_specs = [pl.BlockSpec(memory_space=pl.ANY),
                pl.BlockSpec((k, tn), lambda j, i: (0, j)),
                pl.BlockSpec((k, tn), lambda j, i: (0, j)),
                pl.BlockSpec((wd_rows, d), lambda j, i: (j * n_i + i, 0))]
    out_specs = [pl.BlockSpec((MM_ROWS, tn), lambda j, i: (i, j)),
                 pl.BlockSpec((wd_rows, d), lambda j, i: (j * n_i + i, 0))]
    out_shape = [jax.ShapeDtypeStruct((m, f), BF16), jax.ShapeDtypeStruct((f, d), BF16)]
    args = [x, wg, wu, wd]
    n_side = 0
    if mod is not None:
        cond, w_mod, b_mod2, col0 = mod
        tmod = 128
        n_side = (w_mod.shape[1] - col0) // tmod
        assert n_side <= (f // tn) * n_i
        side = lambda j, i: (0, col0 // tmod + jnp.minimum(j * n_i + i, n_side - 1))
        in_specs += [pl.BlockSpec(cond.shape, lambda j, i: (0, 0)),
                     pl.BlockSpec((w_mod.shape[0], tmod), side),
                     pl.BlockSpec((1, tmod), side)]
        out_specs.append(pl.BlockSpec((cond.shape[0], tmod),
                                      lambda j, i: (0, jnp.minimum(j * n_i + i, n_side - 1))))
        out_shape.append(jax.ShapeDtypeStruct((cond.shape[0], n_side * tmod), F32))
        args += [cond, w_mod, b_mod2]
    return pl.pallas_call(
        functools.partial(_ffn_up_kernel, n_side),
        grid=(f // tn, n_i),
        in_specs=in_specs,
        out_specs=out_specs,
        out_shape=out_shape,
        scratch_shapes=[pltpu.VMEM((X_RING, MM_ROWS, k), BF16), pltpu.SemaphoreType.DMA((X_RING,))]
                       + [pltpu.VMEM((k, tn), BF16)] * 2,
        compiler_params=_params("arbitrary", "arbitrary"),
        name="ffn_up_mod" if mod is not None else "ffn_up",
    )(*args)


def _down_kernel(a_ref, w_ref, o_ref):
    o_ref[...] = jnp.dot(a_ref[...], w_ref[...], preferred_element_type=F32)


def _down(a, w):
    m, f = a.shape
    d = w.shape[1]
    tn = MXU_COLS
    return pl.pallas_call(
        _down_kernel,
        grid=(m // MM_ROWS, d // tn),
        in_specs=[pl.BlockSpec((MM_ROWS, f), lambda i, j: (i, 0)),
                  pl.BlockSpec((f, tn), lambda i, j: (0, j))],
        out_specs=pl.BlockSpec((MM_ROWS, tn), lambda i, j: (i, j)),
        out_shape=jax.ShapeDtypeStruct((m, d), F32),
        compiler_params=_params("parallel", "arbitrary"),
        name="ffn_down",
    )(a, w)


def _softmax_with_sink(s, sink):
    m = jnp.maximum(jnp.max(s, axis=-1, keepdims=True), sink)
    p = jnp.exp(s - m)
    denom = jnp.sum(p, axis=-1, keepdims=True) + jnp.exp(sink - m)
    return p * (1.0 / denom)


def _qk(q, k):
    return lax.dot_general(q, k, (((1,), (1,)), ((), ())), preferred_element_type=F32) * ATTN_SCALE


def _group_sinks(sink_ref, h, rows_per_head):
    head = lax.broadcasted_iota(jnp.int32, (GQA_GROUP * rows_per_head, 1), 0) // rows_per_head
    col = jnp.full(head.shape, sink_ref[h * GQA_GROUP], F32)
    for g in range(1, GQA_GROUP):
        col = jnp.where(head == g, sink_ref[h * GQA_GROUP + g], col)
    return col


def _stack_heads(q_ref, prep):
    return jnp.concatenate([prep(q_ref[:, g * HEAD_DIM:(g + 1) * HEAD_DIM]).astype(BF16)
                            for g in range(GQA_GROUP)], axis=0)


def _unstack_heads(o, o_ref):
    rows = o_ref.shape[0]
    for g in range(GQA_GROUP):
        o_ref[:, g * HEAD_DIM:(g + 1) * HEAD_DIM] = o[g * rows:(g + 1) * rows, :]


def _ctx_attn_kernel(sink_ref, q_ref, k_ref, v_ref, g_ref, o_ref, ks_ref, vs_ref, acc_ref):
    ks_ref[...] = k_ref[...]
    vs_ref[...] = v_ref[...]
    qw = GQA_GROUP * HEAD_DIM
    for h in range(N_KV_HEADS):
        kv_cols = slice(h * HEAD_DIM, (h + 1) * HEAD_DIM)
        q = _stack_heads(q_ref.at[:, h * qw:(h + 1) * qw], lambda x: x)
        s = _qk(q, k_ref[:, kv_cols].astype(BF16))
        p = _softmax_with_sink(s, _group_sinks(sink_ref, h, q_ref.shape[0]))
        o = jnp.dot(p.astype(BF16), v_ref[:, kv_cols].astype(BF16), preferred_element_type=F32)
        _unstack_heads(o, acc_ref.at[:, h * qw:(h + 1) * qw])
    o_ref[...] = _rms(acc_ref[...], g_ref[...]).astype(o_ref.dtype)


def _ctx_attention(z, sink, g_a, n_batch, seq):
    q_w = N_Q_HEADS * HEAD_DIM
    kv_w = N_KV_HEADS * HEAD_DIM
    state = jax.ShapeDtypeStruct((n_batch, seq, kv_w), F32)
    return pl.pallas_call(
        _ctx_attn_kernel,
        grid=(n_batch,),
        in_specs=[pl.BlockSpec(memory_space=pltpu.SMEM),
                  pl.BlockSpec((seq, q_w), lambda b: (b, 0)),
                  pl.BlockSpec((seq, kv_w), lambda b: (b, q_w // kv_w)),
                  pl.BlockSpec((seq, kv_w), lambda b: (b, q_w // kv_w + 1)),
                  pl.BlockSpec((1, q_w), lambda b: (0, 0))],
        out_specs=[pl.BlockSpec((seq, q_w), lambda b: (b, 0)),
                   pl.BlockSpec((None, seq, kv_w), lambda b: (b, 0, 0)),
                   pl.BlockSpec((None, seq, kv_w), lambda b: (b, 0, 0))],
        out_shape=[jax.ShapeDtypeStruct((n_batch * seq, q_w), BF16), state, state],
        scratch_shapes=[pltpu.VMEM((seq, q_w), F32)],
        compiler_params=_params("parallel"),
        name="ctx_attention",
    )(sink, z, z, z, g_a.reshape(1, q_w))


def _rope(x, cos, sin_signed):
    lane = lax.broadcasted_iota(jnp.int32, x.shape, 1)
    first = (lane % ROPE_AXIS_DIM) < ROPE_PAIRS
    partner = jnp.where(first, pltpu.roll(x, HEAD_DIM - ROPE_PAIRS, axis=1),
                        pltpu.roll(x, ROPE_PAIRS, axis=1))
    return x * cos + partner * sin_signed


def _lat_attn_kernel(sink_ref, q_ref, k_ref, v_ref, kc_ref, vc_ref, cos_ref, sin_ref, g_ref, o_ref,
                     kr_ref, vb_ref, kcb_ref, vcb_ref, acc_ref):
    i = pl.program_id(1)
    n_blk = pl.num_programs(1)
    blk = q_ref.shape[0]
    qw = GQA_GROUP * HEAD_DIM

    @pl.when(i == 0)
    def _():
        for h in range(N_KV_HEADS):
            kv_cols = slice(h * HEAD_DIM, (h + 1) * HEAD_DIM)
            kr_ref[:, kv_cols] = _rope(k_ref[:, kv_cols], cos_ref[...], sin_ref[...]).astype(BF16)
        vb_ref[...] = v_ref[...].astype(BF16)
        kcb_ref[...] = kc_ref[...].astype(BF16)
        vcb_ref[...] = vc_ref[...].astype(BF16)

    q0 = pl.multiple_of(i * blk, blk)
    w0 = pl.multiple_of(jnp.clip(i - 1, 0, n_blk - 3) * blk, blk)
    cos = cos_ref[pl.ds(q0, blk), :]
    sin = sin_ref[pl.ds(q0, blk), :]
    rows = GQA_GROUP * blk
    q_pos = q0 + lax.broadcasted_iota(jnp.int32, (rows, 3 * blk), 0) % blk
    k_pos = w0 + lax.broadcasted_iota(jnp.int32, (rows, 3 * blk), 1)
    in_window = jnp.abs(k_pos - q_pos) <= WINDOW
    for h in range(N_KV_HEADS):
        kv_cols = slice(h * HEAD_DIM, (h + 1) * HEAD_DIM)
        q = _stack_heads(q_ref.at[:, h * qw:(h + 1) * qw], lambda x: _rope(x, cos, sin))
        s = jnp.concatenate([jnp.where(in_window, _qk(q, kr_ref[pl.ds(w0, 3 * blk), kv_cols]), NEG),
                             _qk(q, kcb_ref[:, kv_cols])], axis=-1)
        p = _softmax_with_sink(s, _group_sinks(sink_ref, h, blk)).astype(BF16)
        o = (jnp.dot(p[:, :3 * blk], vb_ref[pl.ds(w0, 3 * blk), kv_cols], preferred_element_type=F32)
             + jnp.dot(p[:, 3 * blk:], vcb_ref[:, kv_cols], preferred_element_type=F32))
        _unstack_heads(o, acc_ref.at[:, h * qw:(h + 1) * qw])
    o_ref[...] = _rms(acc_ref[...], g_ref[...]).astype(o_ref.dtype)


def _lat_attention(z, cache_k, cache_v, sink, g_a, cos, sin_signed, row0, n_batch, seq):
    blk = WINDOW
    n_blk = seq // blk
    past = cache_k.shape[1]
    q_w = N_Q_HEADS * HEAD_DIM
    kv_w = N_KV_HEADS * HEAD_DIM
    return pl.pallas_call(
        _lat_attn_kernel,
        grid=(n_batch, n_blk),
        in_specs=[pl.BlockSpec(memory_space=pltpu.SMEM),
                  pl.BlockSpec((blk, q_w), lambda b, i: (row0 // blk + b * n_blk + i, 0)),
                  pl.BlockSpec((seq, kv_w), lambda b, i: (row0 // seq + b, q_w // kv_w)),
                  pl.BlockSpec((seq, kv_w), lambda b, i: (row0 // seq + b, q_w // kv_w + 1)),
                  pl.BlockSpec((None, past, kv_w), lambda b, i: (b, 0, 0)),
                  pl.BlockSpec((None, past, kv_w), lambda b, i: (b, 0, 0)),
                  pl.BlockSpec((seq, HEAD_DIM), lambda b, i: (0, 0)),
                  pl.BlockSpec((seq, HEAD_DIM), lambda b, i: (0, 0)),
                  pl.BlockSpec((1, q_w), lambda b, i: (0, 0))],
        out_specs=pl.BlockSpec((blk, q_w), lambda b, i: (b * n_blk + i, 0)),
        out_shape=jax.ShapeDtypeStruct((n_batch * seq, q_w), BF16),
        scratch_shapes=[pltpu.VMEM((seq, kv_w), BF16), pltpu.VMEM((seq, kv_w), BF16),
                        pltpu.VMEM((past, kv_w), BF16), pltpu.VMEM((past, kv_w), BF16),
                        pltpu.VMEM((blk, q_w), F32)],
        compiler_params=_params("parallel", "arbitrary"),
        name="latent_attention",
    )(sink, z, z, z, cache_k, cache_v, cos, sin_signed, g_a.reshape(1, q_w))


def _rope_tables(seq):
    t = np.arange(seq)
    pos = np.stack([t // GRID_W, t % GRID_W], axis=1).astype(np.float64)
    inv_freq = ROPE_THETA ** (-np.arange(ROPE_PAIRS, dtype=np.float64) / ROPE_PAIRS)
    ang = pos[:, :, None] * inv_freq[None, None, :]
    cos, sin = np.cos(ang), np.sin(ang)
    cos = np.concatenate([cos, cos], axis=-1).reshape(seq, HEAD_DIM)
    sin = np.concatenate([-sin, sin], axis=-1).reshape(seq, HEAD_DIM)
    return jnp.asarray(cos, F32), jnp.asarray(sin, F32)


def _conv_norm_kernel(n_ctx, lat_seq, gb_ref, u_ref, up_ref, un_ref, w_ref, gc_ref, o_ref):
    t = pl.program_id(0)
    rows = u_ref.shape[0]
    r0 = t * rows
    is_ctx = r0 < n_ctx
    is_first = jnp.logical_or(is_ctx, (r0 - n_ctx) % lat_seq == 0)
    is_last = jnp.logical_or(is_ctx, (r0 + rows - n_ctx) % lat_seq == 0)
    u = u_ref[...]
    u_before = jnp.where(is_first, 0.0, up_ref[pl.ds(up_ref.shape[0] - 1, 1), :])
    u_after = jnp.where(is_last, 0.0, un_ref[pl.ds(0, 1), :])
    row = lax.broadcasted_iota(jnp.int32, u.shape, 0)
    u_prev = jnp.where(row == 0, u_before, pltpu.roll(u, 1, axis=0))
    u_next = jnp.where(row == rows - 1, u_after, pltpu.roll(u, rows - 1, axis=0))
    y = u_prev * w_ref[pl.ds(0, 1), :] + u * w_ref[pl.ds(1, 1), :] + u_next * w_ref[pl.ds(2, 1), :]
    o_ref[...] = _rms(gb_ref[...] * y, gc_ref[...]).astype(o_ref.dtype)


def _conv_norm(gb, u, w_conv, g_c, n_ctx, ctx_seq, lat_seq):
    assert ctx_seq == ROW_TILE
    n, wc = u.shape
    halo = 8
    per = ROW_TILE // halo
    last_halo = n // halo - 1
    return pl.pallas_call(
        functools.partial(_conv_norm_kernel, n_ctx, lat_seq),
        grid=(n // ROW_TILE,),
        in_specs=[pl.BlockSpec((ROW_TILE, wc), lambda t: (t, 0)),
                  pl.BlockSpec((ROW_TILE, wc), lambda t: (t, 0)),
                  pl.BlockSpec((halo, wc), lambda t: (jnp.maximum(t * per - 1, 0), 0)),
                  pl.BlockSpec((halo, wc), lambda t: (jnp.minimum((t + 1) * per, last_halo), 0)),
                  pl.BlockSpec((w_conv.shape[0], wc), lambda t: (0, 0)),
                  pl.BlockSpec((1, wc), lambda t: (0, 0))],
        out_specs=pl.BlockSpec((ROW_TILE, wc), lambda t: (t, 0)),
        out_shape=jax.ShapeDtypeStruct((n, wc), BF16),
        compiler_params=_params("parallel"),
        name="conv_norm",
    )(gb, u, u, u, w_conv, g_c.reshape(1, wc))


def kernel(x_prompt, x_sample, c, cache_k, cache_v, c_ctx, w_mod, b_mod, g_pre, g_post, w_in, w_conv,
           sink, g_attn_out, g_conv_out, w_o, w_ffn1_gate, w_ffn1_up, w_ffn1_down,
           w_ffn2_gate, w_ffn2_up, w_ffn2_down):
    batch, seq, d = x_prompt.shape
    dec_batch, dec_seq, _ = x_sample.shape
    depth = w_mod.shape[0]
    past = cache_k.shape[2]
    n_ctx = batch * seq
    n_lat = dec_batch * dec_seq
    attn_w = N_Q_HEADS * HEAD_DIM
    kv_w = N_KV_HEADS * HEAD_DIM
    conv_w = d - attn_w
    d_ff = w_ffn1_gate.shape[2]

    hs = (x_prompt.reshape(n_ctx, d), x_sample.reshape(n_lat, d))
    cond = jnp.concatenate([c_ctx[None, :], c, jnp.zeros((8 - 1 - dec_batch, d), F32)], axis=0)
    cos, sin_signed = _rope_tables(dec_seq)
    post = functools.partial(_post, n_ctx=n_ctx, lat_seq=dec_seq)
    wide = 2 * MXU_COLS

    ks_new, vs_new = [], []
    for l in range(depth):
        b_mod2 = b_mod[l].reshape(1, -1)
        mod_head = _modulation(cond, w_mod[l], b_mod2, 2 * d)
        u = _pre(0, hs, mod_head.reshape(8, 2, d), g_pre[l], n_ctx, dec_seq)
        a, wd, mod_tail = _ffn_up(u, w_ffn1_gate[l], w_ffn1_up[l], w_ffn1_down[l],
                                  mod=(cond, w_mod[l], b_mod2, 2 * d))
        mod3 = jnp.concatenate([mod_head, mod_tail], axis=1).reshape(8, 3 * N_SUB, d)
        o = _down(a, wd)
        h, u = post(0, 0.5, hs, o, mod3, g_post[l], g_pre[l], with_pre=True)

        qkv_w = attn_w + 2 * kv_w
        (zqkv,) = _matmul_ring(u, [(w_in[l], 0)], _identity_combine, [F32], qkv_w, wide, "in_proj_qkv")
        gb, uc = _matmul([u], [(w_in[l], qkv_w), (w_in[l], qkv_w + conv_w), (w_in[l], qkv_w + 2 * conv_w)],
                         _conv_gate_combine, [F32, F32], conv_w, MXU_COLS, "in_proj_conv")
        attn_ctx, k_new, v_new = _ctx_attention(zqkv, sink[l], g_attn_out[l], batch, seq)
        attn_lat = _lat_attention(zqkv, cache_k[:, l].reshape(dec_batch, past, kv_w),
                                  cache_v[:, l].reshape(dec_batch, past, kv_w),
                                  sink[l], g_attn_out[l], cos, sin_signed, n_ctx, dec_batch, dec_seq)
        conv_n = _conv_norm(gb, uc, w_conv[l], g_conv_out[l], n_ctx, seq, dec_seq)
        (o,) = _matmul([(attn_ctx, attn_lat), conv_n], [(w_o[l], 0)], _identity_combine, [F32], d, wide,
                       "out_proj")
        h, u = post(1, 1.0, (h,), o, mod3, g_post[l], g_pre[l], with_pre=True)

        a, wd = _ffn_up(u, w_ffn2_gate[l], w_ffn2_up[l], w_ffn2_down[l])
        o = _down(a, wd)
        if l + 1 < depth:
            (h,) = post(2, 0.5, (h,), o, mod3, g_post[l], g_pre[l], with_pre=False)
            hs = (h,)
        ks_new.append(k_new.reshape(batch, seq, N_KV_HEADS, HEAD_DIM))
        vs_new.append(v_new.reshape(batch, seq, N_KV_HEADS, HEAD_DIM))

    (y_prompt,) = post(2, 0.5, (h,), o, mod3, g_post[depth - 1], g_pre[depth - 1], with_pre=False,
                       row0=0, rows=n_ctx)
    (y_sample,) = post(2, 0.5, (h,), o, mod3, g_post[depth - 1], g_pre[depth - 1], with_pre=False,
                       row0=n_ctx, rows=n_lat)
    return (y_prompt.reshape(batch, seq, d), y_sample.reshape(dec_batch, dec_seq, d),
            jnp.stack(ks_new, axis=1), jnp.stack(vs_new, axis=1))
```
